```python
import math
import jax, jax.numpy as jnp
from jax import lax
import numpy as np

D_MODEL = 1024
BATCH = 16
SEQ = 2048
DEPTH = 1

CHUNK = 64
Q_BLOCK = 128
N_MEM = 256
N_HEADS_A = 8
HEAD_DIM_A = 64
D_A = N_HEADS_A * 2 * HEAD_DIM_A
D_CONV = D_MODEL
CONV_K = 31
N_HEADS_X = 4
HEAD_DIM_X = 256
D_X = N_HEADS_X * HEAD_DIM_X
N_BRANCH = 3
N_BUCKETS = 32
MAX_DIST = 128
EPS = 1e-6
NEG_INF = -1e30
IN_SIZES = (D_A, D_A, D_A, D_A, 2 * D_CONV, D_CONV, D_X, D_X, N_BRANCH * D_MODEL)
IN_WIDTH = sum(IN_SIZES)
IN_SPLITS = tuple(int(s) for s in np.cumsum(IN_SIZES)[:-1])

kernel_name = "hybrid_diffattn_conformer_memxattn_gated"


def _rmsnorm(x, g):
    xf = x.astype(jnp.float32)
    y = xf * lax.rsqrt(jnp.mean(xf * xf, axis=-1, keepdims=True) + EPS)
    return (y * g.astype(jnp.float32)).astype(x.dtype)


def _layernorm(x, g, b):
    xf = x.astype(jnp.float32)
    mu = jnp.mean(xf, axis=-1, keepdims=True)
    var = jnp.mean(jnp.square(xf - mu), axis=-1, keepdims=True)
    y = (xf - mu) * lax.rsqrt(var + EPS)
    return (y * g.astype(jnp.float32) + b.astype(jnp.float32)).astype(x.dtype)


def _t5_bucket(rel):
    nb = N_BUCKETS // 2
    max_exact = nb // 2
    ret = (rel > 0).astype(jnp.int32) * nb
    n = jnp.abs(rel)
    nf = jnp.maximum(n, 1).astype(jnp.float32)
    large = max_exact + (jnp.log(nf / max_exact) / math.log(MAX_DIST / max_exact)
                         * (nb - max_exact)).astype(jnp.int32)
    large = jnp.minimum(large, nb - 1)
    return ret + jnp.where(n < max_exact, n, large)


def _diff_attention(q, k, v, rel_bias, lam):
    S = q.shape[1]
    scale = HEAD_DIM_A ** -0.5
    outs = []
    for i in range(S // Q_BLOCK):
        q0 = i * Q_BLOCK
        lk = q0 + Q_BLOCK
        qb, kb, vb = q[:, q0:lk], k[:, :lk], v[:, :lk]
        qpos = q0 + jnp.arange(Q_BLOCK, dtype=jnp.int32)
        kpos = jnp.arange(lk, dtype=jnp.int32)
        rel = kpos[None, :] - qpos[:, None]
        bias = jnp.transpose(rel_bias[_t5_bucket(rel)].astype(jnp.float32), (2, 0, 1))
        allowed = (kpos[None, :] // CHUNK) <= (qpos[:, None] // CHUNK)
        s = jnp.einsum('bqhcd,bkhcd->bhcqk', qb, kb).astype(jnp.float32) * scale
        s = jnp.where(allowed, s + bias[None, :, None], NEG_INF)
        p = jax.nn.softmax(s, axis=-1)
        a = p[:, :, 0] - lam * p[:, :, 1]
        outs.append(jnp.einsum('bhqk,bkhe->bqhe', a.astype(vb.dtype), vb))
    return jnp.concatenate(outs, axis=1)


def _conv_module(u, conv_w, conv_b, ln_g, ln_b):
    a, g = jnp.split(u, 2, axis=-1)
    c = a * jax.nn.sigmoid(g)
    w = conv_w.astype(c.dtype)[:, None, :]
    c = lax.conv_general_dilated(c, w, window_strides=(1,), padding=[(CONV_K - 1, 0)],
                                 dimension_numbers=('NWC', 'WIO', 'NWC'),
                                 feature_group_count=D_CONV) + conv_b
    return jax.nn.silu(_layernorm(c, ln_g, ln_b))


def _cross_attention(q, mem_n, w_mem_kv):
    B, S = q.shape[:2]
    kv = (mem_n @ w_mem_kv).reshape(B, N_MEM, 2, N_HEADS_X, HEAD_DIM_X)
    s = jnp.einsum('bqhd,bkhd->bhqk', q, kv[:, :, 0]).astype(jnp.float32) * HEAD_DIM_X ** -0.5
    p = jax.nn.softmax(s, axis=-1).astype(q.dtype)
    return jnp.einsum('bhqk,bkhd->bqhd', p, kv[:, :, 1]).reshape(B, S, D_X)


def setup_inputs(seed: int = 0) -> dict:
    key = jax.random.key(seed)
    ks = jax.random.split(key, 24)
    f32 = jnp.float32
    L, D = DEPTH, D_MODEL
    nrm = lambda k, shape, s: jax.random.normal(k, shape, f32) * s
    return {
        "x": nrm(ks[0], (BATCH, SEQ, D), 1.0),
        "mem": nrm(ks[1], (BATCH, N_MEM, D), 1.0),
        "rel_bias": nrm(ks[2], (N_BUCKETS, N_HEADS_A), 0.5),
        "g_pre": 1.0 + nrm(ks[3], (L, D), 0.05),
        "g_mem": 1.0 + nrm(ks[4], (L, D), 0.05),
        "w_in": nrm(ks[5], (L, D, IN_WIDTH), D ** -0.5),
        "b_merge": nrm(ks[6], (L, N_BRANCH * D), 0.02),
        "lam_q1": nrm(ks[7], (L, HEAD_DIM_A), 0.1),
        "lam_k1": nrm(ks[8], (L, HEAD_DIM_A), 0.1),
        "lam_q2": nrm(ks[9], (L, HEAD_DIM_A), 0.1),
        "lam_k2": nrm(ks[10], (L, HEAD_DIM_A), 0.1),
        "g_subln": 1.0 + nrm(ks[11], (L, 2 * HEAD_DIM_A), 0.05),
        "w_oa": nrm(ks[12], (L, D_A, D), D_A ** -0.5),
        "conv_w": nrm(ks[13], (L, CONV_K, D_CONV), CONV_K ** -0.5),
        "conv_b": nrm(ks[14], (L, D_CONV), 0.02),
        "ln_g": 1.0 + nrm(ks[15], (L, D_CONV), 0.05),
        "ln_b": nrm(ks[16], (L, D_CONV), 0.02),
        "w_ob": nrm(ks[17], (L, D_CONV, D), D_CONV ** -0.5),
        "w_mem_kv": nrm(ks[18], (L, D, 2 * D_X), D ** -0.5),
        "w_oc": nrm(ks[19], (L, D_X, D), D_X ** -0.5),
        "w_out": nrm(ks[20], (L, D, D), D ** -0.5),
        "g_post": 1.0 + nrm(ks[21], (L, D), 0.05),
    }


def reference(x, mem, rel_bias, g_pre, g_mem, w_in, b_merge, lam_q1, lam_k1, lam_q2, lam_k2,
              g_subln, w_oa, conv_w, conv_b, ln_g, ln_b, w_ob, w_mem_kv, w_oc, w_out, g_post):
    B, S, D = x.shape
    for l in range(DEPTH):
        lam_init = 0.8 - 0.6 * math.exp(-0.3 * l)
        h = _rmsnorm(x, g_pre[l])
        proj = h @ w_in[l]
        qa, ka, va, za, ub, zb, qc, zc, gates = jnp.split(proj, IN_SPLITS, axis=-1)

        lam = (jnp.exp(jnp.sum(lam_q1[l].astype(jnp.float32) * lam_k1[l].astype(jnp.float32)))
               - jnp.exp(jnp.sum(lam_q2[l].astype(jnp.float32) * lam_k2[l].astype(jnp.float32)))
               + lam_init)
        oa = _diff_attention(qa.reshape(B, S, N_HEADS_A, 2, HEAD_DIM_A),
                             ka.reshape(B, S, N_HEADS_A, 2, HEAD_DIM_A),
                             va.reshape(B, S, N_HEADS_A, 2 * HEAD_DIM_A), rel_bias, lam)
        oa = (_rmsnorm(oa, g_subln[l]) * (1.0 - lam_init)).reshape(B, S, D_A)
        ya = (oa * jax.nn.silu(za)) @ w_oa[l]

        ob = _conv_module(ub, conv_w[l], conv_b[l], ln_g[l], ln_b[l])
        yb = (ob * jax.nn.silu(zb)) @ w_ob[l]

        oc = _cross_attention(qc.reshape(B, S, N_HEADS_X, HEAD_DIM_X), _rmsnorm(mem, g_mem[l]), w_mem_kv[l])
        yc = (oc * jax.nn.silu(zc)) @ w_oc[l]

        g = jax.nn.sigmoid(gates + b_merge[l]).reshape(B, S, N_BRANCH, D)
        y = (g[:, :, 0] * ya + g[:, :, 1] * yb + g[:, :, 2] * yc) @ w_out[l]
        x = x + _rmsnorm(y, g_post[l])
    return x
```

```python
import functools
import math

import jax
import jax.numpy as jnp
import numpy as np
from jax import lax
from jax.experimental import pallas as pl
from jax.experimental.pallas import tpu as pltpu

CHUNK = 64
N_HEADS_A = 8
HEAD_DIM_A = 64
N_HEADS_X = 4
HEAD_DIM_X = 256
CONV_K = 31
N_BRANCH = 3
N_BUCKETS = 32
MAX_DIST = 128
EPS = 1e-6
NEG_INF = -1e30

LANES = 128
SUBLANES = 8
VMEM_LIMIT_BYTES = 48 * 1024 * 1024

PROJ_TM = 1024
PROJ_TN = 2048
ATT_T = 256
CONV_TS = 512
CONV_HALO = 32
CONV_RS = 32
XATT_TQ = 512
MERGE_TM = 512

BF16 = jnp.bfloat16
F32 = jnp.float32


def _cparams(sem):
    return pltpu.CompilerParams(dimension_semantics=sem, vmem_limit_bytes=VMEM_LIMIT_BYTES)


def _silu(z):
    return z * jax.nn.sigmoid(z)


def _proj_kernel(x_ref, g_ref, w_ref, o_ref, h_scr):
    @pl.when(pl.program_id(1) == 0)
    def _():
        xf = x_ref[...]
        ms = jnp.mean(xf * xf, axis=-1, keepdims=True)
        h_scr[...] = (xf * lax.rsqrt(ms + EPS) * g_ref[...]).astype(BF16)

    o_ref[...] = jnp.dot(h_scr[...], w_ref[...], preferred_element_type=F32).astype(o_ref.dtype)


def _proj(x2, g_pre, w_in):
    T, D = x2.shape
    N = w_in.shape[1]
    return pl.pallas_call(
        _proj_kernel,
        grid=(T // PROJ_TM, N // PROJ_TN),
        in_specs=[
            pl.BlockSpec((PROJ_TM, D), lambda i, j: (i, 0)),
            pl.BlockSpec((1, D), lambda i, j: (0, 0)),
            pl.BlockSpec((D, PROJ_TN), lambda i, j: (0, j)),
        ],
        out_specs=pl.BlockSpec((PROJ_TM, PROJ_TN), lambda i, j: (i, j)),
        out_shape=jax.ShapeDtypeStruct((T, N), BF16),
        scratch_shapes=[pltpu.VMEM((PROJ_TM, D), BF16)],
        compiler_params=_cparams(("parallel", "arbitrary")),
        name="proj",
    )(x2, g_pre, w_in)


def _t5_bucket(rel):
    nb = N_BUCKETS // 2
    max_exact = nb // 2
    ret = (rel > 0).astype(jnp.int32) * nb
    n = jnp.abs(rel)
    nf = jnp.maximum(n, 1).astype(jnp.float32)
    large = max_exact + (jnp.log(nf / max_exact) / math.log(MAX_DIST / max_exact)
                         * (nb - max_exact)).astype(jnp.int32)
    large = jnp.minimum(large, nb - 1)
    return ret + jnp.where(n < max_exact, n, large)


def _bucket_tiles():
    qpos = jnp.arange(ATT_T, dtype=jnp.int32)[:, None]
    kpos = jnp.arange(ATT_T, dtype=jnp.int32)[None, :]
    tiles = []
    for d in (0, -1, -2):
        rel = kpos + d * ATT_T - qpos
        tiles.append(_t5_bucket(rel))
    allowed = (kpos // CHUNK) <= (qpos // CHUNK)
    tiles[0] = jnp.where(allowed, tiles[0], -1)
    return jnp.stack(tiles)


def _bias_kernel(relb_ref, bkt_ref, o_ref):
    h = pl.program_id(0)
    bkt = bkt_ref[...]
    res = jnp.zeros(bkt.shape, F32)
    for b in range(N_BUCKETS):
        res = jnp.where(bkt == b, relb_ref[b, h], res)
    o_ref[0] = jnp.where(bkt < 0, NEG_INF, res)


def _bias_tiles(rel_bias):
    bkt = _bucket_tiles()
    return pl.pallas_call(
        _bias_kernel,
        grid=(N_HEADS_A,),
        in_specs=[
            pl.BlockSpec(memory_space=pltpu.SMEM),
            pl.BlockSpec((3, ATT_T, ATT_T), lambda h: (0, 0, 0)),
        ],
        out_specs=pl.BlockSpec((1, 3, ATT_T, ATT_T), lambda h: (h, 0, 0, 0)),
        out_shape=jax.ShapeDtypeStruct((N_HEADS_A, 3, ATT_T, ATT_T), F32),
        compiler_params=_cparams(("arbitrary",)),
        name="bias_tiles",
    )(rel_bias, bkt)


def _attn_kernel(lamv_ref, gs_ref, q_ref, k_ref, v_ref, z_ref, bias_ref, o_ref,
                 acc_scr, m_scr, l_scr, *, lam_init):
    T = ATT_T
    qi = pl.program_id(2)

    q = q_ref[...] * jnp.asarray(HEAD_DIM_A ** -0.5, BF16)
    lane = lax.broadcasted_iota(jnp.int32, q.shape, 1)
    zero = jnp.zeros_like(q)
    qq = jnp.concatenate([jnp.where(lane < HEAD_DIM_A, q, zero),
                          jnp.where(lane >= HEAD_DIM_A, q, zero)], axis=0)

    m_scr[...] = jnp.full(m_scr.shape, -jnp.inf, F32)
    l_scr[...] = jnp.zeros(l_scr.shape, F32)
    acc_scr[...] = jnp.zeros(acc_scr.shape, F32)

    def step(k0, tile):
        kb = k_ref[pl.ds(k0, T), :]
        vb = v_ref[pl.ds(k0, T), :]
        s = lax.dot_general(qq, kb, (((1,), (1,)), ((), ())), preferred_element_type=F32)
        bias = bias_ref[0, tile]
        s = s + jnp.concatenate([bias, bias], axis=0)
        m_prev = m_scr[...]
        m_new = jnp.maximum(m_prev, jnp.max(s, axis=-1, keepdims=True))
        alpha = jnp.exp(m_prev - m_new)
        p = jnp.exp(s - m_new)
        l_scr[...] = alpha * l_scr[...] + jnp.sum(p, axis=-1, keepdims=True)
        acc_scr[...] = alpha * acc_scr[...] + jnp.dot(p.astype(BF16), vb, preferred_element_type=F32)
        m_scr[...] = m_new

    def far_body(j, carry):
        step(pl.multiple_of(j * T, T), 2)
        return carry

    lax.fori_loop(0, jnp.maximum(qi - 1, 0), far_body, 0)

    @pl.when(qi >= 1)
    def _():
        step(pl.multiple_of((qi - 1) * T, T), 1)

    step(pl.multiple_of(qi * T, T), 0)

    lv = lamv_ref[...]
    lam = (jnp.exp(jnp.sum(lv[0:1] * lv[1:2], axis=-1, keepdims=True))
           - jnp.exp(jnp.sum(lv[2:3] * lv[3:4], axis=-1, keepdims=True)) + lam_init)
    o = acc_scr[...] / l_scr[...]
    o = o[:T] - lam * o[T:]
    ms = jnp.mean(o * o, axis=-1, keepdims=True)
    on = o * lax.rsqrt(ms + EPS) * gs_ref[...] * (1.0 - lam_init)
    o_ref[...] = (on * _silu(z_ref[...].astype(F32))).astype(o_ref.dtype)


def _diff_attention(proj, bias_tiles, lamv, g_subln, B, S, lam_init):
    T = ATT_T
    nq = S // T
    H = N_HEADS_A
    d_a = H * 2 * HEAD_DIM_A
    return pl.pallas_call(
        functools.partial(_attn_kernel, lam_init=lam_init),
        grid=(B, H, nq),
        in_specs=[
            pl.BlockSpec((4, HEAD_DIM_A), lambda b, h, i: (0, 0)),
            pl.BlockSpec((1, LANES), lambda b, h, i: (0, 0)),
            pl.BlockSpec((T, LANES), lambda b, h, i: (b * nq + i, h)),
            pl.BlockSpec((S, LANES), lambda b, h, i: (b, H + h)),
            pl.BlockSpec((S, LANES), lambda b, h, i: (b, 2 * H + h)),
            pl.BlockSpec((T, LANES), lambda b, h, i: (b * nq + i, 3 * H + h)),
            pl.BlockSpec((1, 3, T, T), lambda b, h, i: (h, 0, 0, 0)),
        ],
        out_specs=pl.BlockSpec((T, LANES), lambda b, h, i: (b * nq + i, h)),
        out_shape=jax.ShapeDtypeStruct((B * S, d_a), BF16),
        scratch_shapes=[pltpu.VMEM((2 * T, LANES), F32),
                        pltpu.VMEM((2 * T, 1), F32),
                        pltpu.VMEM((2 * T, 1), F32)],
        compiler_params=_cparams(("parallel", "parallel", "arbitrary")),
        name="diff_attn",
    )(lamv, g_subln, proj, proj, proj, proj, bias_tiles)


def _conv_kernel(u_ref, halo_ref, z_ref, cw_ref, cb_ref, lng_ref, lnb_ref, o_ref, c_scr):
    C = o_ref.shape[1]
    TS = o_ref.shape[0]

    def glu(u):
        return u[:, :C].astype(F32) * jax.nn.sigmoid(u[:, C:].astype(F32))

    halo = glu(halo_ref[...])
    c_scr[0, 0:CONV_HALO] = jnp.where(pl.program_id(1) == 0, jnp.zeros_like(halo), halo)
    c_scr[0, CONV_HALO:CONV_HALO + TS] = glu(u_ref[...])
    c_scr[0, CONV_HALO + TS:] = jnp.zeros((SUBLANES, C), F32)

    def shift(r, carry):
        r0 = pl.multiple_of(r * CONV_RS, CONV_RS)
        win = c_scr[0, pl.ds(r0, CONV_RS + SUBLANES), :]
        for s in range(1, SUBLANES):
            c_scr[s, pl.ds(r0, CONV_RS), :] = win[s:s + CONV_RS]
        return carry

    lax.fori_loop(0, (CONV_HALO + TS) // CONV_RS, shift, 0)

    base = CONV_HALO - (CONV_K - 1)

    def chunk(r, carry):
        r0 = pl.multiple_of(r * CONV_RS, CONV_RS)
        acc = jnp.broadcast_to(cb_ref[...], (CONV_RS, C))
        for j in range(CONV_K):
            off = base + j
            tap = c_scr[off % SUBLANES, pl.ds(r0 + off - off % SUBLANES, CONV_RS), :]
            acc = acc + cw_ref[j:j + 1, :] * tap
        mu = jnp.mean(acc, axis=-1, keepdims=True)
        xc = acc - mu
        var = jnp.mean(xc * xc, axis=-1, keepdims=True)
        y = xc * lax.rsqrt(var + EPS) * lng_ref[...] + lnb_ref[...]
        z = z_ref[pl.ds(r0, CONV_RS), :].astype(F32)
        o_ref[pl.ds(r0, CONV_RS), :] = (_silu(y) * _silu(z)).astype(o_ref.dtype)
        return carry

    lax.fori_loop(0, TS // CONV_RS, chunk, 0)


def _conv_module(proj, conv_w, conv_b, ln_g, ln_b, B, S, u_off, z_off):
    C = conv_w.shape[1]
    TS = CONV_TS
    ns = S // TS
    hb = TS // CONV_HALO
    return pl.pallas_call(
        _conv_kernel,
        grid=(B, ns),
        in_specs=[
            pl.BlockSpec((TS, 2 * C), lambda b, s: (b * ns + s, u_off // (2 * C))),
            pl.BlockSpec((CONV_HALO, 2 * C),
                         lambda b, s: (jnp.maximum((b * ns + s) * hb - 1, 0), u_off // (2 * C))),
            pl.BlockSpec((TS, C), lambda b, s: (b * ns + s, z_off // C)),
            pl.BlockSpec((CONV_K, C), lambda b, s: (0, 0)),
            pl.BlockSpec((1, C), lambda b, s: (0, 0)),
            pl.BlockSpec((1, C), lambda b, s: (0, 0)),
            pl.BlockSpec((1, C), lambda b, s: (0, 0)),
        ],
        out_specs=pl.BlockSpec((TS, C), lambda b, s: (b * ns + s, 0)),
        out_shape=jax.ShapeDtypeStruct((B * S, C), BF16),
        scratch_shapes=[pltpu.VMEM((SUBLANES, CONV_HALO + TS + SUBLANES, C), F32)],
        compiler_params=_cparams(("parallel", "parallel")),
        name="conv_module",
    )(proj, proj, proj, conv_w, conv_b, ln_g, ln_b)


def _memkv_kernel(mem_ref, g_ref, w_ref, o_ref):
    xf = mem_ref[...]
    ms = jnp.mean(xf * xf, axis=-1, keepdims=True)
    mn = (xf * lax.rsqrt(ms + EPS) * g_ref[...]).astype(BF16)
    o_ref[...] = jnp.dot(mn, w_ref[...], preferred_element_type=F32).astype(o_ref.dtype)


def _mem_kv(mem2, g_mem, w_mem_kv, n_mem):
    TM, D = mem2.shape
    N = w_mem_kv.shape[1]
    return pl.pallas_call(
        _memkv_kernel,
        grid=(TM // n_mem,),
        in_specs=[
            pl.BlockSpec((n_mem, D), lambda b: (b, 0)),
            pl.BlockSpec((1, D), lambda b: (0, 0)),
            pl.BlockSpec((D, N), lambda b: (0, 0)),
        ],
        out_specs=pl.BlockSpec((n_mem, N), lambda b: (b, 0)),
        out_shape=jax.ShapeDtypeStruct((TM, N), BF16),
        compiler_params=_cparams(("parallel",)),
        name="mem_kv",
    )(mem2, g_mem, w_mem_kv)


def _xattn_kernel(q_ref, z_ref, kv_ref, o_ref):
    dx = HEAD_DIM_X
    d_x = N_HEADS_X * dx
    scale = jnp.asarray(dx ** -0.5, BF16)
    for h in range(N_HEADS_X):
        qh = q_ref[:, h * dx:(h + 1) * dx] * scale
        kh = kv_ref[:, h * dx:(h + 1) * dx]
        vh = kv_ref[:, d_x + h * dx:d_x + (h + 1) * dx]
        s = lax.dot_general(qh, kh, (((1,), (1,)), ((), ())), preferred_element_type=F32)
        m = jnp.max(s, axis=-1, keepdims=True)
        p = jnp.exp(s - m)
        l = jnp.sum(p, axis=-1, keepdims=True)
        o = jnp.dot(p.astype(BF16), vh, preferred_element_type=F32) / l
        z = z_ref[:, h * dx:(h + 1) * dx].astype(F32)
        o_ref[:, h * dx:(h + 1) * dx] = (o * _silu(z)).astype(o_ref.dtype)


def _cross_attention(proj, kv, B, S, n_mem, q_off, z_off):
    d_x = N_HEADS_X * HEAD_DIM_X
    TQ = XATT_TQ
    nq = S // TQ
    return pl.pallas_call(
        _xattn_kernel,
        grid=(B, nq),
        in_specs=[
            pl.BlockSpec((TQ, d_x), lambda b, i: (b * nq + i, q_off // d_x)),
            pl.BlockSpec((TQ, d_x), lambda b, i: (b * nq + i, z_off // d_x)),
            pl.BlockSpec((n_mem, 2 * d_x), lambda b, i: (b, 0)),
        ],
        out_specs=pl.BlockSpec((TQ, d_x), lambda b, i: (b * nq + i, 0)),
        out_shape=jax.ShapeDtypeStruct((B * S, d_x), BF16),
        compiler_params=_cparams(("parallel", "parallel")),
        name="mem_xattn",
    )(proj, proj, kv)


def _merge_kernel(a_ref, b_ref, c_ref, gt_ref, x_ref, woa_ref, wob_ref, woc_ref, wout_ref,
                  bm_ref, gp_ref, o_ref):
    D = x_ref.shape[1]
    merged = jnp.zeros(x_ref.shape, F32)
    for n, (br_ref, w_ref) in enumerate(((a_ref, woa_ref), (b_ref, wob_ref), (c_ref, woc_ref))):
        yb = jnp.dot(br_ref[...], w_ref[...], preferred_element_type=F32)
        gate = jax.nn.sigmoid(gt_ref[:, n * D:(n + 1) * D].astype(F32) + bm_ref[:, n * D:(n + 1) * D])
        merged = merged + gate * yb
    y = jnp.dot(merged.astype(BF16), wout_ref[...], preferred_element_type=F32)
    ms = jnp.mean(y * y, axis=-1, keepdims=True)
    o_ref[...] = x_ref[...] + y * lax.rsqrt(ms + EPS) * gp_ref[...]


def _merge(ya, yb, yc, proj, x2, w_oa, w_ob, w_oc, w_out, b_merge, g_post, g_off):
    T, D = x2.shape
    TM = MERGE_TM
    row = lambda i: (i, 0)
    fixed = lambda i: (0, 0)
    return pl.pallas_call(
        _merge_kernel,
        grid=(T // TM,),
        in_specs=[
            pl.BlockSpec((TM, D), row),
            pl.BlockSpec((TM, D), row),
            pl.BlockSpec((TM, D), row),
            pl.BlockSpec((TM, N_BRANCH * D), lambda i: (i, g_off // (N_BRANCH * D))),
            pl.BlockSpec((TM, D), row),
            pl.BlockSpec((D, D), fixed),
            pl.BlockSpec((D, D), fixed),
            pl.BlockSpec((D, D), fixed),
            pl.BlockSpec((D, D), fixed),
            pl.BlockSpec((1, N_BRANCH * D), fixed),
            pl.BlockSpec((1, D), fixed),
        ],
        out_specs=pl.BlockSpec((TM, D), row),
        out_shape=jax.ShapeDtypeStruct((T, D), F32),
        compiler_params=_cparams(("parallel",)),
        name="merge_out",
    )(ya, yb, yc, proj, x2, w_oa, w_ob, w_oc, w_out, b_merge, g_post)


def kernel(x, mem, rel_bias, g_pre, g_mem, w_in, b_merge, lam_q1, lam_k1, lam_q2, lam_k2,
           g_subln, w_oa, conv_w, conv_b, ln_g, ln_b, w_ob, w_mem_kv, w_oc, w_out, g_post):
    B, S, D = x.shape
    n_mem = mem.shape[1]
    depth = w_in.shape[0]
    d_a = N_HEADS_A * 2 * HEAD_DIM_A
    d_conv = conv_w.shape[2]
    d_x = N_HEADS_X * HEAD_DIM_X
    sizes = (d_a, d_a, d_a, d_a, 2 * d_conv, d_conv, d_x, d_x, N_BRANCH * D)
    offs = [int(o) for o in np.cumsum((0,) + sizes)]
    assert offs[-1] == w_in.shape[2] and d_a == d_conv == d_x == D
    assert MAX_DIST <= ATT_T and ATT_T % CHUNK == 0 and CONV_HALO >= CONV_K - 1

    x2 = x.reshape(B * S, D)
    mem2 = mem.reshape(B * n_mem, D)
    bias_tiles = _bias_tiles(rel_bias)
    for l in range(depth):
        lam_init = 0.8 - 0.6 * math.exp(-0.3 * l)
        proj = _proj(x2, g_pre[l][None], w_in[l].astype(BF16))

        lamv = jnp.stack([lam_q1[l], lam_k1[l], lam_q2[l], lam_k2[l]])
        ya = _diff_attention(proj, bias_tiles, lamv, g_subln[l][None], B, S, lam_init)

        yb = _conv_module(proj, conv_w[l], conv_b[l][None], ln_g[l][None], ln_b[l][None],
                          B, S, offs[4], offs[5])

        kv = _mem_kv(mem2, g_mem[l][None], w_mem_kv[l].astype(BF16), n_mem)
        yc = _cross_attention(proj, kv, B, S, n_mem, offs[6], offs[7])

        x2 = _merge(ya, yb, yc, proj, x2, w_oa[l].astype(BF16), w_ob[l].astype(BF16),
                    w_oc[l].astype(BF16), w_out[l].astype(BF16), b_merge[l][None], g_post[l][None],
                    offs[8])
    return x2.reshape(B, S, D)
```

```python
import functools
import math

import jax
import jax.numpy as jnp
import numpy as np
from jax import lax
from jax.experimental import pallas as pl
from jax.experimental.pallas import tpu as pltpu

CHUNK = 64
N_HEADS_A = 8
HEAD_DIM_A = 64
N_HEADS_X = 4
HEAD_DIM_X = 256
CONV_K = 31
N_BRANCH = 3
N_BUCKETS = 32
MAX_DIST = 128
EPS = 1e-6
NEG_INF = -1e30

LANES = 128
SUBLANES = 8
VMEM_LIMIT_BYTES = 48 * 1024 * 1024

PROJ_TM = 1024
PROJ_TN = 2048
ATT_T = 256
CONV_TS = 512
CONV_HALO = 32
CONV_RS = 32
XATT_TQ = 512
MERGE_TM = 512

BF16 = jnp.bfloat16
F32 = jnp.float32


def _cparams(sem):
    return pltpu.CompilerParams(dimension_semantics=sem, vmem_limit_bytes=VMEM_LIMIT_BYTES)


def _silu(z):
    return z * jax.nn.sigmoid(z)


def _proj_kernel(x_ref, g_ref, w_ref, o_ref, h_scr):
    @pl.when(pl.program_id(1) == 0)
    def _():
        xf = x_ref[...]
        ms = jnp.mean(xf * xf, axis=-1, keepdims=True)
        h_scr[...] = (xf * lax.rsqrt(ms + EPS) * g_ref[...]).astype(BF16)

    o_ref[...] = jnp.dot(h_scr[...], w_ref[...], preferred_element_type=F32).astype(o_ref.dtype)


def _proj(x2, g_pre, w_in):
    T, D = x2.shape
    N = w_in.shape[1]
    return pl.pallas_call(
        _proj_kernel,
        grid=(T // PROJ_TM, N // PROJ_TN),
        in_specs=[
            pl.BlockSpec((PROJ_TM, D), lambda i, j: (i, 0)),
            pl.BlockSpec((1, D), lambda i, j: (0, 0)),
            pl.BlockSpec((D, PROJ_TN), lambda i, j: (0, j)),
        ],
        out_specs=pl.BlockSpec((PROJ_TM, PROJ_TN), lambda i, j: (i, j)),
        out_shape=jax.ShapeDtypeStruct((T, N), BF16),
        scratch_shapes=[pltpu.VMEM((PROJ_TM, D), BF16)],
        compiler_params=_cparams(("parallel", "arbitrary")),
        name="proj",
    )(x2, g_pre, w_in)


def _t5_bucket(rel):
    nb = N_BUCKETS // 2
    max_exact = nb // 2
    ret = (rel > 0).astype(jnp.int32) * nb
    n = jnp.abs(rel)
    nf = jnp.maximum(n, 1).astype(jnp.float32)
    large = max_exact + (jnp.log(nf / max_exact) / math.log(MAX_DIST / max_exact)
                         * (nb - max_exact)).astype(jnp.int32)
    large = jnp.minimum(large, nb - 1)
    return ret + jnp.where(n < max_exact, n, large)


def _bucket_tiles():
    kpos = jnp.arange(ATT_T, dtype=jnp.int32)[:, None]
    qpos = jnp.arange(ATT_T, dtype=jnp.int32)[None, :]
    tiles = []
    for d in (0, -1, -2):
        rel = kpos + d * ATT_T - qpos
        tiles.append(_t5_bucket(rel))
    allowed = (kpos // CHUNK) <= (qpos // CHUNK)
    tiles[0] = jnp.where(allowed, tiles[0], -1)
    return jnp.stack(tiles)


def _bias_kernel(relb_ref, bkt_ref, o_ref):
    h = pl.program_id(0)
    bkt = bkt_ref[...]
    res = jnp.zeros(bkt.shape, F32)
    for b in range(N_BUCKETS):
        res = jnp.where(bkt == b, relb_ref[b, h], res)
    near = res[0:2] - res[2, 0:1, 0:1]
    o_ref[0] = jnp.where(bkt[0:2] < 0, NEG_INF, near)


def _bias_tiles(rel_bias):
    bkt = _bucket_tiles()
    return pl.pallas_call(
        _bias_kernel,
        grid=(N_HEADS_A,),
        in_specs=[
            pl.BlockSpec(memory_space=pltpu.SMEM),
            pl.BlockSpec((3, ATT_T, ATT_T), lambda h: (0, 0, 0)),
        ],
        out_specs=pl.BlockSpec((1, 2, ATT_T, ATT_T), lambda h: (h, 0, 0, 0)),
        out_shape=jax.ShapeDtypeStruct((N_HEADS_A, 2, ATT_T, ATT_T), F32),
        compiler_params=_cparams(("arbitrary",)),
        name="bias_tiles",
    )(rel_bias, bkt)


def _attn_kernel(lamv_ref, gs_ref, q_ref, k_ref, v_ref, z_ref, bias_ref, o_ref,
                 acc_scr, m_scr, l_scr, *, lam_init):
    T = ATT_T
    qi = pl.program_id(2)

    q = q_ref[...] * jnp.asarray(HEAD_DIM_A ** -0.5, BF16)
    lane = lax.broadcasted_iota(jnp.int32, q.shape, 1)
    zero = jnp.zeros_like(q)
    qq = jnp.concatenate([jnp.where(lane < HEAD_DIM_A, q, zero),
                          jnp.where(lane >= HEAD_DIM_A, q, zero)], axis=0)

    m_scr[...] = jnp.full(m_scr.shape, -jnp.inf, F32)
    l_scr[...] = jnp.zeros(l_scr.shape, F32)
    acc_scr[...] = jnp.zeros(acc_scr.shape, F32)
    ones = jnp.ones((SUBLANES, T), BF16)

    def step(k0, tile):
        kb = k_ref[pl.ds(k0, T), :]
        vb = v_ref[pl.ds(k0, T), :]
        st = lax.dot_general(kb, qq, (((1,), (1,)), ((), ())), preferred_element_type=F32)
        if tile is not None:
            bias = bias_ref[0, tile]
            st = st + jnp.concatenate([bias, bias], axis=1)
        m_prev = m_scr[...]
        m_new = jnp.maximum(m_prev, jnp.max(st, axis=0, keepdims=True))
        alpha = jnp.exp(m_prev - m_new)
        p = jnp.exp(st - m_new).astype(BF16)
        l_scr[...] = alpha * l_scr[...] + jnp.dot(ones, p, preferred_element_type=F32)[0:1]
        pv = lax.dot_general(vb, p, (((0,), (0,)), ((), ())), preferred_element_type=F32)
        acc_scr[...] = alpha * acc_scr[...] + pv
        m_scr[...] = m_new

    def far_body(j, carry):
        step(pl.multiple_of(j * T, T), None)
        return carry

    lax.fori_loop(0, jnp.maximum(qi - 1, 0), far_body, 0)

    @pl.when(qi >= 1)
    def _():
        step(pl.multiple_of((qi - 1) * T, T), 1)

    step(pl.multiple_of(qi * T, T), 0)

    lv = lamv_ref[...]
    lam = (jnp.exp(jnp.sum(lv[0:1] * lv[1:2], axis=-1, keepdims=True))
           - jnp.exp(jnp.sum(lv[2:3] * lv[3:4], axis=-1, keepdims=True)) + lam_init)
    ot = acc_scr[...] / l_scr[...]
    ot = ot[:, :T] - lam * ot[:, T:]
    ms = jnp.mean(ot * ot, axis=0, keepdims=True)
    on = (ot * lax.rsqrt(ms + EPS)).T * gs_ref[...] * (1.0 - lam_init)
    o_ref[...] = (on * _silu(z_ref[...].astype(F32))).astype(o_ref.dtype)


def _diff_attention(proj, bias_tiles, lamv, g_subln, B, S, lam_init):
    T = ATT_T
    nq = S // T
    H = N_HEADS_A
    d_a = H * 2 * HEAD_DIM_A
    return pl.pallas_call(
        functools.partial(_attn_kernel, lam_init=lam_init),
        grid=(B, H, nq),
        in_specs=[
            pl.BlockSpec((4, HEAD_DIM_A), lambda b, h, i: (0, 0)),
            pl.BlockSpec((1, LANES), lambda b, h, i: (0, 0)),
            pl.BlockSpec((T, LANES), lambda b, h, i: (b * nq + i, h)),
            pl.BlockSpec((S, LANES), lambda b, h, i: (b, H + h)),
            pl.BlockSpec((S, LANES), lambda b, h, i: (b, 2 * H + h)),
            pl.BlockSpec((T, LANES), lambda b, h, i: (b * nq + i, 3 * H + h)),
            pl.BlockSpec((1, 2, T, T), lambda b, h, i: (h, 0, 0, 0)),
        ],
        out_specs=pl.BlockSpec((T, LANES), lambda b, h, i: (b * nq + i, h)),
        out_shape=jax.ShapeDtypeStruct((B * S, d_a), BF16),
        scratch_shapes=[pltpu.VMEM((2 * HEAD_DIM_A, 2 * T), F32),
                        pltpu.VMEM((1, 2 * T), F32),
                        pltpu.VMEM((1, 2 * T), F32)],
        compiler_params=_cparams(("parallel", "parallel", "arbitrary")),
        name="diff_attn",
    )(lamv, g_subln, proj, proj, proj, proj, bias_tiles)


def _conv_kernel(u_ref, halo_ref, z_ref, cw_ref, cb_ref, lng_ref, lnb_ref, o_ref, c_scr):
    C = o_ref.shape[1]
    TS = o_ref.shape[0]

    def glu(u):
        return u[:, :C].astype(F32) * jax.nn.sigmoid(u[:, C:].astype(F32))

    halo = glu(halo_ref[...])
    c_scr[0, 0:CONV_HALO] = jnp.where(pl.program_id(1) == 0, jnp.zeros_like(halo), halo)
    c_scr[0, CONV_HALO:CONV_HALO + TS] = glu(u_ref[...])
    c_scr[0, CONV_HALO + TS:] = jnp.zeros((SUBLANES, C), F32)

    def shift(r, carry):
        r0 = pl.multiple_of(r * CONV_RS, CONV_RS)
        win = c_scr[0, pl.ds(r0, CONV_RS + SUBLANES), :]
        for s in range(1, SUBLANES):
            c_scr[s, pl.ds(r0, CONV_RS), :] = win[s:s + CONV_RS]
        return carry

    lax.fori_loop(0, (CONV_HALO + TS) // CONV_RS, shift, 0)

    base = CONV_HALO - (CONV_K - 1)

    def chunk(r, carry):
        r0 = pl.multiple_of(r * CONV_RS, CONV_RS)
        acc = jnp.broadcast_to(cb_ref[...], (CONV_RS, C))
        for j in range(CONV_K):
            off = base + j
            tap = c_scr[off % SUBLANES, pl.ds(r0 + off - off % SUBLANES, CONV_RS), :]
            acc = acc + cw_ref[j:j + 1, :] * tap
        mu = jnp.mean(acc, axis=-1, keepdims=True)
        xc = acc - mu
        var = jnp.mean(xc * xc, axis=-1, keepdims=True)
        y = xc * lax.rsqrt(var + EPS) * lng_ref[...] + lnb_ref[...]
        z = z_ref[pl.ds(r0, CONV_RS), :].astype(F32)
        o_ref[pl.ds(r0, CONV_RS), :] = (_silu(y) * _silu(z)).astype(o_ref.dtype)
        return carry

    lax.fori_loop(0, TS // CONV_RS, chunk, 0)


def _conv_module(proj, conv_w, conv_b, ln_g, ln_b, B, S, u_off, z_off):
    C = conv_w.shape[1]
    TS = CONV_TS
    ns = S // TS
    hb = TS // CONV_HALO
    return pl.pallas_call(
        _conv_kernel,
        grid=(B, ns),
        in_specs=[
            pl.BlockSpec((TS, 2 * C), lambda b, s: (b * ns + s, u_off // (2 * C))),
            pl.BlockSpec((CONV_HALO, 2 * C),
                         lambda b, s: (jnp.maximum((b * ns + s) * hb - 1, 0), u_off // (2 * C))),
            pl.BlockSpec((TS, C), lambda b, s: (b * ns + s, z_off // C)),
            pl.BlockSpec((CONV_K, C), lambda b, s: (0, 0)),
            pl.BlockSpec((1, C), lambda b, s: (0, 0)),
            pl.BlockSpec((1, C), lambda b, s: (0, 0)),
            pl.BlockSpec((1, C), lambda b, s: (0, 0)),
        ],
        out_specs=pl.BlockSpec((TS, C), lambda b, s: (b * ns + s, 0)),
        out_shape=jax.ShapeDtypeStruct((B * S, C), BF16),
        scratch_shapes=[pltpu.VMEM((SUBLANES, CONV_HALO + TS + SUBLANES, C), F32)],
        compiler_params=_cparams(("parallel", "parallel")),
        name="conv_module",
    )(proj, proj, proj, conv_w, conv_b, ln_g, ln_b)


def _memkv_kernel(mem_ref, g_ref, w_ref, o_ref):
    xf = mem_ref[...]
    ms = jnp.mean(xf * xf, axis=-1, keepdims=True)
    mn = (xf * lax.rsqrt(ms + EPS) * g_ref[...]).astype(BF16)
    o_ref[...] = jnp.dot(mn, w_ref[...], preferred_element_type=F32).astype(o_ref.dtype)


def _mem_kv(mem2, g_mem, w_mem_kv, n_mem):
    TM, D = mem2.shape
    N = w_mem_kv.shape[1]
    return pl.pallas_call(
        _memkv_kernel,
        grid=(TM // n_mem,),
        in_specs=[
            pl.BlockSpec((n_mem, D), lambda b: (b, 0)),
            pl.BlockSpec((1, D), lambda b: (0, 0)),
            pl.BlockSpec((D, N), lambda b: (0, 0)),
        ],
        out_specs=pl.BlockSpec((n_mem, N), lambda b: (b, 0)),
        out_shape=jax.ShapeDtypeStruct((TM, N), BF16),
        compiler_params=_cparams(("parallel",)),
        name="mem_kv",
    )(mem2, g_mem, w_mem_kv)


def _xattn_kernel(q_ref, z_ref, kv_ref, o_ref):
    dx = HEAD_DIM_X
    d_x = N_HEADS_X * dx
    scale = jnp.asarray(dx ** -0.5, BF16)
    for h in range(N_HEADS_X):
        qh = q_ref[:, h * dx:(h + 1) * dx] * scale
        kh = kv_ref[:, h * dx:(h + 1) * dx]
        vh = kv_ref[:, d_x + h * dx:d_x + (h + 1) * dx]
        s = lax.dot_general(qh, kh, (((1,), (1,)), ((), ())), preferred_element_type=F32)
        m = jnp.max(s, axis=-1, keepdims=True)
        p = jnp.exp(s - m)
        l = jnp.sum(p, axis=-1, keepdims=True)
        o = jnp.dot(p.astype(BF16), vh, preferred_element_type=F32) / l
        z = z_ref[:, h * dx:(h + 1) * dx].astype(F32)
        o_ref[:, h * dx:(h + 1) * dx] = (o * _silu(z)).astype(o_ref.dtype)


def _cross_attention(proj, kv, B, S, n_mem, q_off, z_off):
    d_x = N_HEADS_X * HEAD_DIM_X
    TQ = XATT_TQ
    nq = S // TQ
    return pl.pallas_call(
        _xattn_kernel,
        grid=(B, nq),
        in_specs=[
            pl.BlockSpec((TQ, d_x), lambda b, i: (b * nq + i, q_off // d_x)),
            pl.BlockSpec((TQ, d_x), lambda b, i: (b * nq + i, z_off // d_x)),
            pl.BlockSpec((n_mem, 2 * d_x), lambda b, i: (b, 0)),
        ],
        out_specs=pl.BlockSpec((TQ, d_x), lambda b, i: (b * nq + i, 0)),
        out_shape=jax.ShapeDtypeStruct((B * S, d_x), BF16),
        compiler_params=_cparams(("parallel", "parallel")),
        name="mem_xattn",
    )(proj, proj, kv)


def _merge_kernel(a_ref, b_ref, c_ref, gt_ref, x_ref, woa_ref, wob_ref, woc_ref, wout_ref,
                  bm_ref, gp_ref, o_ref):
    D = x_ref.shape[1]
    merged = jnp.zeros(x_ref.shape, F32)
    for n, (br_ref, w_ref) in enumerate(((a_ref, woa_ref), (b_ref, wob_ref), (c_ref, woc_ref))):
        yb = jnp.dot(br_ref[...], w_ref[...], preferred_element_type=F32)
        gate = jax.nn.sigmoid(gt_ref[:, n * D:(n + 1) * D].astype(F32) + bm_ref[:, n * D:(n + 1) * D])
        merged = merged + gate * yb
    y = jnp.dot(merged.astype(BF16), wout_ref[...], preferred_element_type=F32)
    ms = jnp.mean(y * y, axis=-1, keepdims=True)
    o_ref[...] = x_ref[...] + y * lax.rsqrt(ms + EPS) * gp_ref[...]


def _merge(ya, yb, yc, proj, x2, w_oa, w_ob, w_oc, w_out, b_merge, g_post, g_off):
    T, D = x2.shape
    TM = MERGE_TM
    row = lambda i: (i, 0)
    fixed = lambda i: (0, 0)
    return pl.pallas_call(
        _merge_kernel,
        grid=(T // TM,),
        in_specs=[
            pl.BlockSpec((TM, D), row),
            pl.BlockSpec((TM, D), row),
            pl.BlockSpec((TM, D), row),
            pl.BlockSpec((TM, N_BRANCH * D), lambda i: (i, g_off // (N_BRANCH * D))),
            pl.BlockSpec((TM, D), row),
            pl.BlockSpec((D, D), fixed),
            pl.BlockSpec((D, D), fixed),
            pl.BlockSpec((D, D), fixed),
            pl.BlockSpec((D, D), fixed),
            pl.BlockSpec((1, N_BRANCH * D), fixed),
            pl.BlockSpec((1, D), fixed),
        ],
        out_specs=pl.BlockSpec((TM, D), row),
        out_shape=jax.ShapeDtypeStruct((T, D), F32),
        compiler_params=_cparams(("parallel",)),
        name="merge_out",
    )(ya, yb, yc, proj, x2, w_oa, w_ob, w_oc, w_out, b_merge, g_post)


def kernel(x, mem, rel_bias, g_pre, g_mem, w_in, b_merge, lam_q1, lam_k1, lam_q2, lam_k2,
           g_subln, w_oa, conv_w, conv_b, ln_g, ln_b, w_ob, w_mem_kv, w_oc, w_out, g_post):
    B, S, D = x.shape
    n_mem = mem.shape[1]
    depth = w_in.shape[0]
    d_a = N_HEADS_A * 2 * HEAD_DIM_A
    d_conv = conv_w.shape[2]
    d_x = N_HEADS_X * HEAD_DIM_X
    sizes = (d_a, d_a, d_a, d_a, 2 * d_conv, d_conv, d_x, d_x, N_BRANCH * D)
    offs = [int(o) for o in np.cumsum((0,) + sizes)]
    assert offs[-1] == w_in.shape[2] and d_a == d_conv == d_x == D
    assert MAX_DIST <= ATT_T and ATT_T % CHUNK == 0 and CONV_HALO >= CONV_K - 1

    x2 = x.reshape(B * S, D)
    mem2 = mem.reshape(B * n_mem, D)
    bias_tiles = _bias_tiles(rel_bias)
    for l in range(depth):
        lam_init = 0.8 - 0.6 * math.exp(-0.3 * l)
        proj = _proj(x2, g_pre[l][None], w_in[l].astype(BF16))

        lamv = jnp.stack([lam_q1[l], lam_k1[l], lam_q2[l], lam_k2[l]])
        ya = _diff_attention(proj, bias_tiles, lamv, g_subln[l][None], B, S, lam_init)

        yb = _conv_module(proj, conv_w[l], conv_b[l][None], ln_g[l][None], ln_b[l][None],
                          B, S, offs[4], offs[5])

        kv = _mem_kv(mem2, g_mem[l][None], w_mem_kv[l].astype(BF16), n_mem)
        yc = _cross_attention(proj, kv, B, S, n_mem, offs[6], offs[7])

        x2 = _merge(ya, yb, yc, proj, x2, w_oa[l].astype(BF16), w_ob[l].astype(BF16),
                    w_oc[l].astype(BF16), w_out[l].astype(BF16), b_merge[l][None], g_post[l][None],
                    offs[8])
    return x2.reshape(B, S, D)
```

```python
import functools
import math

import jax
import jax.numpy as jnp
import numpy as np
from jax import lax
from jax.experimental import pallas as pl
from jax.experimental.pallas import tpu as pltpu

CHUNK = 64
N_HEADS_A = 8
HEAD_DIM_A = 64
N_HEADS_X = 4
HEAD_DIM_X = 256
CONV_K = 31
N_BRANCH = 3
N_BUCKETS = 32
MAX_DIST = 128
EPS = 1e-6
NEG_INF = -1e30

LANES = 128
SUBLANES = 8
VMEM_LIMIT_BYTES = 48 * 1024 * 1024

PROJ_TM = 1024
PROJ_TN = 2048
ATT_T = 256
ATT_NH = 2
CONV_TS = 512
CONV_HALO = 32
CONV_RS = 32
XATT_TQ = 512
MERGE_TM = 512

BF16 = jnp.bfloat16
F32 = jnp.float32


def _cparams(sem):
    return pltpu.CompilerParams(dimension_semantics=sem, vmem_limit_bytes=VMEM_LIMIT_BYTES)


def _silu(z):
    return z * jax.nn.sigmoid(z)


def _proj_kernel(x_ref, g_ref, w_ref, o_ref, h_scr):
    @pl.when(pl.program_id(1) == 0)
    def _():
        xf = x_ref[...]
        ms = jnp.mean(xf * xf, axis=-1, keepdims=True)
        h_scr[...] = (xf * lax.rsqrt(ms + EPS) * g_ref[...]).astype(BF16)

    o_ref[...] = jnp.dot(h_scr[...], w_ref[...], preferred_element_type=F32).astype(o_ref.dtype)


def _proj(x2, g_pre, w_in):
    T, D = x2.shape
    N = w_in.shape[1]
    return pl.pallas_call(
        _proj_kernel,
        grid=(T // PROJ_TM, N // PROJ_TN),
        in_specs=[
            pl.BlockSpec((PROJ_TM, D), lambda i, j: (i, 0)),
            pl.BlockSpec((1, D), lambda i, j: (0, 0)),
            pl.BlockSpec((D, PROJ_TN), lambda i, j: (0, j)),
        ],
        out_specs=pl.BlockSpec((PROJ_TM, PROJ_TN), lambda i, j: (i, j)),
        out_shape=jax.ShapeDtypeStruct((T, N), BF16),
        scratch_shapes=[pltpu.VMEM((PROJ_TM, D), BF16)],
        compiler_params=_cparams(("parallel", "arbitrary")),
        name="proj",
    )(x2, g_pre, w_in)


def _t5_bucket(rel):
    nb = N_BUCKETS // 2
    max_exact = nb // 2
    ret = (rel > 0).astype(jnp.int32) * nb
    n = jnp.abs(rel)
    nf = jnp.maximum(n, 1).astype(jnp.float32)
    large = max_exact + (jnp.log(nf / max_exact) / math.log(MAX_DIST / max_exact)
                         * (nb - max_exact)).astype(jnp.int32)
    large = jnp.minimum(large, nb - 1)
    return ret + jnp.where(n < max_exact, n, large)


def _bucket_tiles():
    kpos = jnp.arange(ATT_T, dtype=jnp.int32)[:, None]
    qpos = jnp.arange(ATT_T, dtype=jnp.int32)[None, :]
    tiles = []
    for d in (0, -1, -2):
        rel = kpos + d * ATT_T - qpos
        tiles.append(_t5_bucket(rel))
    allowed = (kpos // CHUNK) <= (qpos // CHUNK)
    tiles[0] = jnp.where(allowed, tiles[0], -1)
    return jnp.stack(tiles)


def _bias_kernel(relb_ref, bkt_ref, o_ref):
    h = pl.program_id(0)
    bkt = bkt_ref[...]
    res = jnp.zeros(bkt.shape, F32)
    for b in range(N_BUCKETS):
        res = jnp.where(bkt == b, relb_ref[b, h], res)
    near = res[0:2] - res[2, 0:1, 0:1]
    o_ref[0] = jnp.where(bkt[0:2] < 0, NEG_INF, near)


def _bias_tiles(rel_bias):
    bkt = _bucket_tiles()
    return pl.pallas_call(
        _bias_kernel,
        grid=(N_HEADS_A,),
        in_specs=[
            pl.BlockSpec(memory_space=pltpu.SMEM),
            pl.BlockSpec((3, ATT_T, ATT_T), lambda h: (0, 0, 0)),
        ],
        out_specs=pl.BlockSpec((1, 2, ATT_T, ATT_T), lambda h: (h, 0, 0, 0)),
        out_shape=jax.ShapeDtypeStruct((N_HEADS_A, 2, ATT_T, ATT_T), F32),
        compiler_params=_cparams(("arbitrary",)),
        name="bias_tiles",
    )(rel_bias, bkt)


def _attn_kernel(lamv_ref, gs_ref, q_ref, k_ref, v_ref, z_ref, bias_ref, o_ref,
                 qq_scr, sta_scr, stb_scr, acc_scr, m_scr, l_scr, *, lam_init):
    T = ATT_T
    qi = pl.program_id(2)

    lane = lax.broadcasted_iota(jnp.int32, (T, LANES), 1)
    for hh in range(ATT_NH):
        q = q_ref[:, hh * LANES:(hh + 1) * LANES] * jnp.asarray(HEAD_DIM_A ** -0.5, BF16)
        zero = jnp.zeros_like(q)
        qq_scr[hh, 0:T] = jnp.where(lane < HEAD_DIM_A, q, zero)
        qq_scr[hh, T:2 * T] = jnp.where(lane >= HEAD_DIM_A, q, zero)

    m_scr[...] = jnp.full(m_scr.shape, -jnp.inf, F32)
    l_scr[...] = jnp.zeros(l_scr.shape, F32)
    acc_scr[...] = jnp.zeros(acc_scr.shape, F32)
    ones = jnp.ones((SUBLANES, T), BF16)

    def scores(t, st_ref):
        k0 = pl.multiple_of(t * T, T)
        for hh in range(ATT_NH):
            kb = k_ref[pl.ds(k0, T), hh * LANES:(hh + 1) * LANES]
            st_ref[hh] = lax.dot_general(kb, qq_scr[hh], (((1,), (1,)), ((), ())),
                                         preferred_element_type=F32)

    def softmax_pv(t, st_ref, tile):
        k0 = pl.multiple_of(t * T, T)
        for hh in range(ATT_NH):
            st = st_ref[hh]
            if tile is not None:
                bias = bias_ref[hh, tile]
                st = st + jnp.concatenate([bias, bias], axis=1)
            m_prev = m_scr[hh]
            m_new = jnp.maximum(m_prev, jnp.max(st, axis=0, keepdims=True))
            alpha = jnp.exp(m_prev - m_new)
            p = jnp.exp(st - m_new).astype(BF16)
            vb = v_ref[pl.ds(k0, T), hh * LANES:(hh + 1) * LANES]
            l_scr[hh] = alpha * l_scr[hh] + jnp.dot(ones, p, preferred_element_type=F32)[0:1]
            pv = lax.dot_general(vb, p, (((0,), (0,)), ((), ())), preferred_element_type=F32)
            acc_scr[hh] = alpha * acc_scr[hh] + pv
            m_scr[hh] = m_new

    def tick(t, cur, nxt, tile):
        scores(t + 1, nxt)
        softmax_pv(t, cur, tile)

    n_far = jnp.maximum(qi - 1, 0)
    scores(0, sta_scr)

    def far_pair(i, carry):
        tick(2 * i, sta_scr, stb_scr, None)
        tick(2 * i + 1, stb_scr, sta_scr, None)
        return carry

    lax.fori_loop(0, n_far // 2, far_pair, 0)
    odd = lax.rem(n_far, 2) == 1

    @pl.when(odd)
    def _():
        tick(n_far - 1, sta_scr, stb_scr, None)

    @pl.when(jnp.logical_and(qi >= 1, jnp.logical_not(odd)))
    def _():
        tick(qi - 1, sta_scr, stb_scr, 1)
        softmax_pv(qi, stb_scr, 0)

    @pl.when(jnp.logical_and(qi >= 1, odd))
    def _():
        tick(qi - 1, stb_scr, sta_scr, 1)
        softmax_pv(qi, sta_scr, 0)

    @pl.when(qi == 0)
    def _():
        softmax_pv(0, sta_scr, 0)

    lv = lamv_ref[...]
    lam = (jnp.exp(jnp.sum(lv[0:1] * lv[1:2], axis=-1, keepdims=True))
           - jnp.exp(jnp.sum(lv[2:3] * lv[3:4], axis=-1, keepdims=True)) + lam_init)
    for hh in range(ATT_NH):
        ot = acc_scr[hh] / l_scr[hh]
        ot = ot[:, :T] - lam * ot[:, T:]
        ms = jnp.mean(ot * ot, axis=0, keepdims=True)
        on = (ot * lax.rsqrt(ms + EPS)).T * gs_ref[...] * (1.0 - lam_init)
        z = z_ref[:, hh * LANES:(hh + 1) * LANES].astype(F32)
        o_ref[:, hh * LANES:(hh + 1) * LANES] = (on * _silu(z)).astype(o_ref.dtype)


def _diff_attention(proj, bias_tiles, lamv, g_subln, B, S, lam_init):
    T = ATT_T
    nq = S // T
    H = N_HEADS_A
    d_a = H * 2 * HEAD_DIM_A
    W = ATT_NH * LANES
    G = H // ATT_NH
    return pl.pallas_call(
        functools.partial(_attn_kernel, lam_init=lam_init),
        grid=(B, G, nq),
        in_specs=[
            pl.BlockSpec((4, HEAD_DIM_A), lambda b, g, i: (0, 0)),
            pl.BlockSpec((1, LANES), lambda b, g, i: (0, 0)),
            pl.BlockSpec((T, W), lambda b, g, i: (b * nq + i, g)),
            pl.BlockSpec((S, W), lambda b, g, i: (b, G + g)),
            pl.BlockSpec((S, W), lambda b, g, i: (b, 2 * G + g)),
            pl.BlockSpec((T, W), lambda b, g, i: (b * nq + i, 3 * G + g)),
            pl.BlockSpec((ATT_NH, 2, T, T), lambda b, g, i: (g, 0, 0, 0)),
        ],
        out_specs=pl.BlockSpec((T, W), lambda b, g, i: (b * nq + i, g)),
        out_shape=jax.ShapeDtypeStruct((B * S, d_a), BF16),
        scratch_shapes=[pltpu.VMEM((ATT_NH, 2 * T, LANES), BF16),
                        pltpu.VMEM((ATT_NH, T, 2 * T), F32),
                        pltpu.VMEM((ATT_NH, T, 2 * T), F32),
                        pltpu.VMEM((ATT_NH, 2 * HEAD_DIM_A, 2 * T), F32),
                        pltpu.VMEM((ATT_NH, 1, 2 * T), F32),
                        pltpu.VMEM((ATT_NH, 1, 2 * T), F32)],
        compiler_params=_cparams(("parallel", "parallel", "arbitrary")),
        name="diff_attn",
    )(lamv, g_subln, proj, proj, proj, proj, bias_tiles)


def _conv_kernel(u_ref, halo_ref, z_ref, cw_ref, cb_ref, lng_ref, lnb_ref, o_ref, c_scr):
    C = o_ref.shape[1]
    TS = o_ref.shape[0]

    def glu(u):
        return u[:, :C].astype(F32) * jax.nn.sigmoid(u[:, C:].astype(F32))

    halo = glu(halo_ref[...])
    c_scr[0, 0:CONV_HALO] = jnp.where(pl.program_id(1) == 0, jnp.zeros_like(halo), halo)
    c_scr[0, CONV_HALO:CONV_HALO + TS] = glu(u_ref[...])
    c_scr[0, CONV_HALO + TS:] = jnp.zeros((SUBLANES, C), F32)

    def shift(r, carry):
        r0 = pl.multiple_of(r * CONV_RS, CONV_RS)
        win = c_scr[0, pl.ds(r0, CONV_RS + SUBLANES), :]
        for s in range(1, SUBLANES):
            c_scr[s, pl.ds(r0, CONV_RS), :] = win[s:s + CONV_RS]
        return carry

    lax.fori_loop(0, (CONV_HALO + TS) // CONV_RS, shift, 0)

    base = CONV_HALO - (CONV_K - 1)

    def chunk(r, carry):
        r0 = pl.multiple_of(r * CONV_RS, CONV_RS)
        acc = jnp.broadcast_to(cb_ref[...], (CONV_RS, C))
        for j in range(CONV_K):
            off = base + j
            tap = c_scr[off % SUBLANES, pl.ds(r0 + off - off % SUBLANES, CONV_RS), :]
            acc = acc + cw_ref[j:j + 1, :] * tap
        mu = jnp.mean(acc, axis=-1, keepdims=True)
        xc = acc - mu
        var = jnp.mean(xc * xc, axis=-1, keepdims=True)
        y = xc * lax.rsqrt(var + EPS) * lng_ref[...] + lnb_ref[...]
        z = z_ref[pl.ds(r0, CONV_RS), :].astype(F32)
        o_ref[pl.ds(r0, CONV_RS), :] = (_silu(y) * _silu(z)).astype(o_ref.dtype)
        return carry

    lax.fori_loop(0, TS // CONV_RS, chunk, 0)


def _conv_module(proj, conv_w, conv_b, ln_g, ln_b, B, S, u_off, z_off):
    C = conv_w.shape[1]
    TS = CONV_TS
    ns = S // TS
    hb = TS // CONV_HALO
    return pl.pallas_call(
        _conv_kernel,
        grid=(B, ns),
        in_specs=[
            pl.BlockSpec((TS, 2 * C), lambda b, s: (b * ns + s, u_off // (2 * C))),
            pl.BlockSpec((CONV_HALO, 2 * C),
                         lambda b, s: (jnp.maximum((b * ns + s) * hb - 1, 0), u_off // (2 * C))),
            pl.BlockSpec((TS, C), lambda b, s: (b * ns + s, z_off // C)),
            pl.BlockSpec((CONV_K, C), lambda b, s: (0, 0)),
            pl.BlockSpec((1, C), lambda b, s: (0, 0)),
            pl.BlockSpec((1, C), lambda b, s: (0, 0)),
            pl.BlockSpec((1, C), lambda b, s: (0, 0)),
        ],
        out_specs=pl.BlockSpec((TS, C), lambda b, s: (b * ns + s, 0)),
        out_shape=jax.ShapeDtypeStruct((B * S, C), BF16),
        scratch_shapes=[pltpu.VMEM((SUBLANES, CONV_HALO + TS + SUBLANES, C), F32)],
        compiler_params=_cparams(("parallel", "parallel")),
        name="conv_module",
    )(proj, proj, proj, conv_w, conv_b, ln_g, ln_b)


def _memkv_kernel(mem_ref, g_ref, w_ref, o_ref):
    xf = mem_ref[...]
    ms = jnp.mean(xf * xf, axis=-1, keepdims=True)
    mn = (xf * lax.rsqrt(ms + EPS) * g_ref[...]).astype(BF16)
    o_ref[...] = jnp.dot(mn, w_ref[...], preferred_element_type=F32).astype(o_ref.dtype)


def _mem_kv(mem2, g_mem, w_mem_kv, n_mem):
    TM, D = mem2.shape
    N = w_mem_kv.shape[1]
    return pl.pallas_call(
        _memkv_kernel,
        grid=(TM // n_mem,),
        in_specs=[
            pl.BlockSpec((n_mem, D), lambda b: (b, 0)),
            pl.BlockSpec((1, D), lambda b: (0, 0)),
            pl.BlockSpec((D, N), lambda b: (0, 0)),
        ],
        out_specs=pl.BlockSpec((n_mem, N), lambda b: (b, 0)),
        out_shape=jax.ShapeDtypeStruct((TM, N), BF16),
        compiler_params=_cparams(("parallel",)),
        name="mem_kv",
    )(mem2, g_mem, w_mem_kv)


def _xattn_kernel(q_ref, z_ref, kv_ref, o_ref):
    dx = HEAD_DIM_X
    d_x = N_HEADS_X * dx
    scale = jnp.asarray(dx ** -0.5, BF16)
    for h in range(N_HEADS_X):
        qh = q_ref[:, h * dx:(h + 1) * dx] * scale
        kh = kv_ref[:, h * dx:(h + 1) * dx]
        vh = kv_ref[:, d_x + h * dx:d_x + (h + 1) * dx]
        s = lax.dot_general(qh, kh, (((1,), (1,)), ((), ())), preferred_element_type=F32)
        m = jnp.max(s, axis=-1, keepdims=True)
        p = jnp.exp(s - m)
        l = jnp.sum(p, axis=-1, keepdims=True)
        o = jnp.dot(p.astype(BF16), vh, preferred_element_type=F32) / l
        z = z_ref[:, h * dx:(h + 1) * dx].astype(F32)
        o_ref[:, h * dx:(h + 1) * dx] = (o * _silu(z)).astype(o_ref.dtype)


def _cross_attention(proj, kv, B, S, n_mem, q_off, z_off):
    d_x = N_HEADS_X * HEAD_DIM_X
    TQ = XATT_TQ
    nq = S // TQ
    return pl.pallas_call(
        _xattn_kernel,
        grid=(B, nq),
        in_specs=[
            pl.BlockSpec((TQ, d_x), lambda b, i: (b * nq + i, q_off // d_x)),
            pl.BlockSpec((TQ, d_x), lambda b, i: (b * nq + i, z_off // d_x)),
            pl.BlockSpec((n_mem, 2 * d_x), lambda b, i: (b, 0)),
        ],
        out_specs=pl.BlockSpec((TQ, d_x), lambda b, i: (b * nq + i, 0)),
        out_shape=jax.ShapeDtypeStruct((B * S, d_x), BF16),
        compiler_params=_cparams(("parallel", "parallel")),
        name="mem_xattn",
    )(proj, proj, kv)


def _merge_kernel(a_ref, b_ref, c_ref, gt_ref, x_ref, woa_ref, wob_ref, woc_ref, wout_ref,
                  bm_ref, gp_ref, o_ref):
    D = x_ref.shape[1]
    merged = jnp.zeros(x_ref.shape, F32)
    for n, (br_ref, w_ref) in enumerate(((a_ref, woa_ref), (b_ref, wob_ref), (c_ref, woc_ref))):
        yb = jnp.dot(br_ref[...], w_ref[...], preferred_element_type=F32)
        gate = jax.nn.sigmoid(gt_ref[:, n * D:(n + 1) * D].astype(F32) + bm_ref[:, n * D:(n + 1) * D])
        merged = merged + gate * yb
    y = jnp.dot(merged.astype(BF16), wout_ref[...], preferred_element_type=F32)
    ms = jnp.mean(y * y, axis=-1, keepdims=True)
    o_ref[...] = x_ref[...] + y * lax.rsqrt(ms + EPS) * gp_ref[...]


def _merge(ya, yb, yc, proj, x2, w_oa, w_ob, w_oc, w_out, b_merge, g_post, g_off):
    T, D = x2.shape
    TM = MERGE_TM
    row = lambda i: (i, 0)
    fixed = lambda i: (0, 0)
    return pl.pallas_call(
        _merge_kernel,
        grid=(T // TM,),
        in_specs=[
            pl.BlockSpec((TM, D), row),
            pl.BlockSpec((TM, D), row),
            pl.BlockSpec((TM, D), row),
            pl.BlockSpec((TM, N_BRANCH * D), lambda i: (i, g_off // (N_BRANCH * D))),
            pl.BlockSpec((TM, D), row),
            pl.BlockSpec((D, D), fixed),
            pl.BlockSpec((D, D), fixed),
            pl.BlockSpec((D, D), fixed),
            pl.BlockSpec((D, D), fixed),
            pl.BlockSpec((1, N_BRANCH * D), fixed),
            pl.BlockSpec((1, D), fixed),
        ],
        out_specs=pl.BlockSpec((TM, D), row),
        out_shape=jax.ShapeDtypeStruct((T, D), F32),
        compiler_params=_cparams(("parallel",)),
        name="merge_out",
    )(ya, yb, yc, proj, x2, w_oa, w_ob, w_oc, w_out, b_merge, g_post)


def kernel(x, mem, rel_bias, g_pre, g_mem, w_in, b_merge, lam_q1, lam_k1, lam_q2, lam_k2,
           g_subln, w_oa, conv_w, conv_b, ln_g, ln_b, w_ob, w_mem_kv, w_oc, w_out, g_post):
    B, S, D = x.shape
    n_mem = mem.shape[1]
    depth = w_in.shape[0]
    d_a = N_HEADS_A * 2 * HEAD_DIM_A
    d_conv = conv_w.shape[2]
    d_x = N_HEADS_X * HEAD_DIM_X
    sizes = (d_a, d_a, d_a, d_a, 2 * d_conv, d_conv, d_x, d_x, N_BRANCH * D)
    offs = [int(o) for o in np.cumsum((0,) + sizes)]
    assert offs[-1] == w_in.shape[2] and d_a == d_conv == d_x == D
    assert MAX_DIST <= ATT_T and ATT_T % CHUNK == 0 and CONV_HALO >= CONV_K - 1

    x2 = x.reshape(B * S, D)
    mem2 = mem.reshape(B * n_mem, D)
    bias_tiles = _bias_tiles(rel_bias)
    for l in range(depth):
        lam_init = 0.8 - 0.6 * math.exp(-0.3 * l)
        proj = _proj(x2, g_pre[l][None], w_in[l].astype(BF16))

        lamv = jnp.stack([lam_q1[l], lam_k1[l], lam_q2[l], lam_k2[l]])
        ya = _diff_attention(proj, bias_tiles, lamv, g_subln[l][None], B, S, lam_init)

        yb = _conv_module(proj, conv_w[l], conv_b[l][None], ln_g[l][None], ln_b[l][None],
                          B, S, offs[4], offs[5])

        kv = _mem_kv(mem2, g_mem[l][None], w_mem_kv[l].astype(BF16), n_mem)
        yc = _cross_attention(proj, kv, B, S, n_mem, offs[6], offs[7])

        x2 = _merge(ya, yb, yc, proj, x2, w_oa[l].astype(BF16), w_ob[l].astype(BF16),
                    w_oc[l].astype(BF16), w_out[l].astype(BF16), b_merge[l][None], g_post[l][None],
                    offs[8])
    return x2.reshape(B, S, D)
```

```python
import functools
import math

import jax
import jax.numpy as jnp
import numpy as np
from jax import lax
from jax.experimental import pallas as pl
from jax.experimental.pallas import tpu as pltpu

CHUNK = 64
N_HEADS_A = 8
HEAD_DIM_A = 64
N_HEADS_X = 4
HEAD_DIM_X = 256
CONV_K = 31
N_BRANCH = 3
N_BUCKETS = 32
MAX_DIST = 128
EPS = 1e-6
NEG_INF = -1e30

LANES = 128
SUBLANES = 8
VMEM_LIMIT_BYTES = 48 * 1024 * 1024

PROJ_TM = 1024
PROJ_TN = 2048
ATT_T = 256
ATT_NH = 4
CONV_TS = 512
CONV_HALO = 32
CONV_RS = 32
XATT_TQ = 512
MERGE_TM = 512

BF16 = jnp.bfloat16
F32 = jnp.float32


def _cparams(sem):
    return pltpu.CompilerParams(dimension_semantics=sem, vmem_limit_bytes=VMEM_LIMIT_BYTES)


def _silu(z):
    return z * jax.nn.sigmoid(z)


def _proj_kernel(x_ref, g_ref, w_ref, o_ref, h_scr):
    @pl.when(pl.program_id(1) == 0)
    def _():
        xf = x_ref[...]
        ms = jnp.mean(xf * xf, axis=-1, keepdims=True)
        h_scr[...] = (xf * lax.rsqrt(ms + EPS) * g_ref[...]).astype(BF16)

    o_ref[...] = jnp.dot(h_scr[...], w_ref[...], preferred_element_type=F32).astype(o_ref.dtype)


def _proj(x2, g_pre, w_in):
    T, D = x2.shape
    N = w_in.shape[1]
    return pl.pallas_call(
        _proj_kernel,
        grid=(T // PROJ_TM, N // PROJ_TN),
        in_specs=[
            pl.BlockSpec((PROJ_TM, D), lambda i, j: (i, 0)),
            pl.BlockSpec((1, D), lambda i, j: (0, 0)),
            pl.BlockSpec((D, PROJ_TN), lambda i, j: (0, j)),
        ],
        out_specs=pl.BlockSpec((PROJ_TM, PROJ_TN), lambda i, j: (i, j)),
        out_shape=jax.ShapeDtypeStruct((T, N), BF16),
        scratch_shapes=[pltpu.VMEM((PROJ_TM, D), BF16)],
        compiler_params=_cparams(("parallel", "arbitrary")),
        name="proj",
    )(x2, g_pre, w_in)


def _t5_bucket(rel):
    nb = N_BUCKETS // 2
    max_exact = nb // 2
    ret = (rel > 0).astype(jnp.int32) * nb
    n = jnp.abs(rel)
    nf = jnp.maximum(n, 1).astype(jnp.float32)
    large = max_exact + (jnp.log(nf / max_exact) / math.log(MAX_DIST / max_exact)
                         * (nb - max_exact)).astype(jnp.int32)
    large = jnp.minimum(large, nb - 1)
    return ret + jnp.where(n < max_exact, n, large)


def _bucket_tiles():
    kpos = jnp.arange(ATT_T, dtype=jnp.int32)[:, None]
    qpos = jnp.arange(ATT_T, dtype=jnp.int32)[None, :]
    tiles = []
    for d in (0, -1, -2):
        rel = kpos + d * ATT_T - qpos
        tiles.append(_t5_bucket(rel))
    allowed = (kpos // CHUNK) <= (qpos // CHUNK)
    tiles[0] = jnp.where(allowed, tiles[0], -1)
    return jnp.stack(tiles)


def _bias_kernel(relb_ref, bkt_ref, o_ref):
    h = pl.program_id(0)
    bkt = bkt_ref[...]
    res = jnp.zeros(bkt.shape, F32)
    for b in range(N_BUCKETS):
        res = jnp.where(bkt == b, relb_ref[b, h], res)
    near = res[0:2] - res[2, 0:1, 0:1]
    o_ref[0] = jnp.where(bkt[0:2] < 0, NEG_INF, near)


def _bias_tiles(rel_bias):
    bkt = _bucket_tiles()
    return pl.pallas_call(
        _bias_kernel,
        grid=(N_HEADS_A,),
        in_specs=[
            pl.BlockSpec(memory_space=pltpu.SMEM),
            pl.BlockSpec((3, ATT_T, ATT_T), lambda h: (0, 0, 0)),
        ],
        out_specs=pl.BlockSpec((1, 2, ATT_T, ATT_T), lambda h: (h, 0, 0, 0)),
        out_shape=jax.ShapeDtypeStruct((N_HEADS_A, 2, ATT_T, ATT_T), F32),
        compiler_params=_cparams(("arbitrary",)),
        name="bias_tiles",
    )(rel_bias, bkt)


def _attn_kernel(lamv_ref, gs_ref, q_ref, k_ref, v_ref, z_ref, bias_ref, o_ref,
                 qq_scr, sta_scr, stb_scr, acc_scr, m_scr, l_scr, *, lam_init):
    T = ATT_T
    qi = pl.program_id(2)

    lane = lax.broadcasted_iota(jnp.int32, (T, LANES), 1)
    for hh in range(ATT_NH):
        q = q_ref[:, hh * LANES:(hh + 1) * LANES] * jnp.asarray(HEAD_DIM_A ** -0.5, BF16)
        zero = jnp.zeros_like(q)
        qq_scr[hh, 0:T] = jnp.where(lane < HEAD_DIM_A, q, zero)
        qq_scr[hh, T:2 * T] = jnp.where(lane >= HEAD_DIM_A, q, zero)

    m_scr[...] = jnp.full(m_scr.shape, -jnp.inf, F32)
    l_scr[...] = jnp.zeros(l_scr.shape, F32)
    acc_scr[...] = jnp.zeros(acc_scr.shape, F32)
    ones = jnp.ones((SUBLANES, T), BF16)

    def scores(t, st_ref):
        k0 = pl.multiple_of(t * T, T)
        for hh in range(ATT_NH):
            kb = k_ref[pl.ds(k0, T), hh * LANES:(hh + 1) * LANES]
            st_ref[hh] = lax.dot_general(kb, qq_scr[hh], (((1,), (1,)), ((), ())),
                                         preferred_element_type=F32)

    def softmax_pv(t, st_ref, tile):
        k0 = pl.multiple_of(t * T, T)
        for hh in range(ATT_NH):
            st = st_ref[hh]
            if tile is not None:
                bias = bias_ref[hh, tile]
                st = st + jnp.concatenate([bias, bias], axis=1)
            m_prev = m_scr[hh]
            m_new = jnp.maximum(m_prev, jnp.max(st, axis=0, keepdims=True))
            alpha = jnp.exp(m_prev - m_new)
            p = jnp.exp(st - m_new).astype(BF16)
            vb = v_ref[pl.ds(k0, T), hh * LANES:(hh + 1) * LANES]
            l_scr[hh] = alpha * l_scr[hh] + jnp.dot(ones, p, preferred_element_type=F32)[0:1]
            pv = lax.dot_general(vb, p, (((0,), (0,)), ((), ())), preferred_element_type=F32)
            acc_scr[hh] = alpha * acc_scr[hh] + pv
            m_scr[hh] = m_new

    def tick(t, cur, nxt, tile):
        scores(t + 1, nxt)
        softmax_pv(t, cur, tile)

    n_far = jnp.maximum(qi - 1, 0)
    scores(0, sta_scr)

    def far_pair(i, carry):
        tick(2 * i, sta_scr, stb_scr, None)
        tick(2 * i + 1, stb_scr, sta_scr, None)
        return carry

    lax.fori_loop(0, n_far // 2, far_pair, 0)
    odd = lax.rem(n_far, 2) == 1

    @pl.when(odd)
    def _():
        tick(n_far - 1, sta_scr, stb_scr, None)

    @pl.when(jnp.logical_and(qi >= 1, jnp.logical_not(odd)))
    def _():
        tick(qi - 1, sta_scr, stb_scr, 1)
        softmax_pv(qi, stb_scr, 0)

    @pl.when(jnp.logical_and(qi >= 1, odd))
    def _():
        tick(qi - 1, stb_scr, sta_scr, 1)
        softmax_pv(qi, sta_scr, 0)

    @pl.when(qi == 0)
    def _():
        softmax_pv(0, sta_scr, 0)


    lv = lamv_ref[...]
    lam = (jnp.exp(jnp.sum(lv[0:1] * lv[1:2], axis=-1, keepdims=True))
           - jnp.exp(jnp.sum(lv[2:3] * lv[3:4], axis=-1, keepdims=True)) + lam_init)
    for hh in range(ATT_NH):
        ot = acc_scr[hh] / l_scr[hh]
        ot = ot[:, :T] - lam * ot[:, T:]
        ms = jnp.mean(ot * ot, axis=0, keepdims=True)
        on = (ot * lax.rsqrt(ms + EPS)).T * gs_ref[...] * (1.0 - lam_init)
        z = z_ref[:, hh * LANES:(hh + 1) * LANES].astype(F32)
        o_ref[:, hh * LANES:(hh + 1) * LANES] = (on * _silu(z)).astype(o_ref.dtype)


def _diff_attention(proj, bias_tiles, lamv, g_subln, B, S, lam_init):
    T = ATT_T
    nq = S // T
    H = N_HEADS_A
    d_a = H * 2 * HEAD_DIM_A
    W = ATT_NH * LANES
    G = H // ATT_NH
    return pl.pallas_call(
        functools.partial(_attn_kernel, lam_init=lam_init),
        grid=(B, G, nq),
        in_specs=[
            pl.BlockSpec((4, HEAD_DIM_A), lambda b, g, i: (0, 0)),
            pl.BlockSpec((1, LANES), lambda b, g, i: (0, 0)),
            pl.BlockSpec((T, W), lambda b, g, i: (b * nq + i, g)),
            pl.BlockSpec((S, W), lambda b, g, i: (b, G + g)),
            pl.BlockSpec((S, W), lambda b, g, i: (b, 2 * G + g)),
            pl.BlockSpec((T, W), lambda b, g, i: (b * nq + i, 3 * G + g)),
            pl.BlockSpec((ATT_NH, 2, T, T), lambda b, g, i: (g, 0, 0, 0)),
        ],
        out_specs=pl.BlockSpec((T, W), lambda b, g, i: (b * nq + i, g)),
        out_shape=jax.ShapeDtypeStruct((B * S, d_a), BF16),
        scratch_shapes=[pltpu.VMEM((ATT_NH, 2 * T, LANES), BF16),
                        pltpu.VMEM((ATT_NH, T, 2 * T), F32),
                        pltpu.VMEM((ATT_NH, T, 2 * T), F32),
                        pltpu.VMEM((ATT_NH, 2 * HEAD_DIM_A, 2 * T), F32),
                        pltpu.VMEM((ATT_NH, 1, 2 * T), F32),
                        pltpu.VMEM((ATT_NH, 1, 2 * T), F32)],
        compiler_params=_cparams(("parallel", "parallel", "arbitrary")),
        name="diff_attn",
    )(lamv, g_subln, proj, proj, proj, proj, bias_tiles)


def _conv_kernel(u_ref, halo_ref, z_ref, cw_ref, cb_ref, lng_ref, lnb_ref, o_ref, c_scr):
    C = o_ref.shape[1]
    TS = o_ref.shape[0]

    def glu(u):
        return u[:, :C].astype(F32) * jax.nn.sigmoid(u[:, C:].astype(F32))

    halo = glu(halo_ref[...])
    c_scr[0, 0:CONV_HALO] = jnp.where(pl.program_id(1) == 0, jnp.zeros_like(halo), halo)
    c_scr[0, CONV_HALO:CONV_HALO + TS] = glu(u_ref[...])
    c_scr[0, CONV_HALO + TS:] = jnp.zeros((SUBLANES, C), F32)

    def shift(r, carry):
        r0 = pl.multiple_of(r * CONV_RS, CONV_RS)
        win = c_scr[0, pl.ds(r0, CONV_RS + SUBLANES), :]
        for s in range(1, SUBLANES):
            c_scr[s, pl.ds(r0, CONV_RS), :] = win[s:s + CONV_RS]
        return carry

    lax.fori_loop(0, (CONV_HALO + TS) // CONV_RS, shift, 0)

    base = CONV_HALO - (CONV_K - 1)

    def chunk(r, carry):
        r0 = pl.multiple_of(r * CONV_RS, CONV_RS)
        acc = jnp.broadcast_to(cb_ref[...], (CONV_RS, C))
        for j in range(CONV_K):
            off = base + j
            tap = c_scr[off % SUBLANES, pl.ds(r0 + off - off % SUBLANES, CONV_RS), :]
            acc = acc + cw_ref[j:j + 1, :] * tap
        mu = jnp.mean(acc, axis=-1, keepdims=True)
        xc = acc - mu
        var = jnp.mean(xc * xc, axis=-1, keepdims=True)
        y = xc * lax.rsqrt(var + EPS) * lng_ref[...] + lnb_ref[...]
        z = z_ref[pl.ds(r0, CONV_RS), :].astype(F32)
        o_ref[pl.ds(r0, CONV_RS), :] = (_silu(y) * _silu(z)).astype(o_ref.dtype)
        return carry

    lax.fori_loop(0, TS // CONV_RS, chunk, 0)


def _conv_module(proj, conv_w, conv_b, ln_g, ln_b, B, S, u_off, z_off):
    C = conv_w.shape[1]
    TS = CONV_TS
    ns = S // TS
    hb = TS // CONV_HALO
    return pl.pallas_call(
        _conv_kernel,
        grid=(B, ns),
        in_specs=[
            pl.BlockSpec((TS, 2 * C), lambda b, s: (b * ns + s, u_off // (2 * C))),
            pl.BlockSpec((CONV_HALO, 2 * C),
                         lambda b, s: (jnp.maximum((b * ns + s) * hb - 1, 0), u_off // (2 * C))),
            pl.BlockSpec((TS, C), lambda b, s: (b * ns + s, z_off // C)),
            pl.BlockSpec((CONV_K, C), lambda b, s: (0, 0)),
            pl.BlockSpec((1, C), lambda b, s: (0, 0)),
            pl.BlockSpec((1, C), lambda b, s: (0, 0)),
            pl.BlockSpec((1, C), lambda b, s: (0, 0)),
        ],
        out_specs=pl.BlockSpec((TS, C), lambda b, s: (b * ns + s, 0)),
        out_shape=jax.ShapeDtypeStruct((B * S, C), BF16),
        scratch_shapes=[pltpu.VMEM((SUBLANES, CONV_HALO + TS + SUBLANES, C), F32)],
        compiler_params=_cparams(("parallel", "parallel")),
        name="conv_module",
    )(proj, proj, proj, conv_w, conv_b, ln_g, ln_b)


def _memkv_kernel(mem_ref, g_ref, w_ref, o_ref):
    xf = mem_ref[...]
    ms = jnp.mean(xf * xf, axis=-1, keepdims=True)
    mn = (xf * lax.rsqrt(ms + EPS) * g_ref[...]).astype(BF16)
    o_ref[...] = jnp.dot(mn, w_ref[...], preferred_element_type=F32).astype(o_ref.dtype)


def _mem_kv(mem2, g_mem, w_mem_kv, n_mem):
    TM, D = mem2.shape
    N = w_mem_kv.shape[1]
    return pl.pallas_call(
        _memkv_kernel,
        grid=(TM // n_mem,),
        in_specs=[
            pl.BlockSpec((n_mem, D), lambda b: (b, 0)),
            pl.BlockSpec((1, D), lambda b: (0, 0)),
            pl.BlockSpec((D, N), lambda b: (0, 0)),
        ],
        out_specs=pl.BlockSpec((n_mem, N), lambda b: (b, 0)),
        out_shape=jax.ShapeDtypeStruct((TM, N), BF16),
        compiler_params=_cparams(("parallel",)),
        name="mem_kv",
    )(mem2, g_mem, w_mem_kv)


def _xattn_kernel(q_ref, z_ref, kv_ref, o_ref):
    dx = HEAD_DIM_X
    d_x = N_HEADS_X * dx
    scale = jnp.asarray(dx ** -0.5, BF16)
    for h in range(N_HEADS_X):
        qh = q_ref[:, h * dx:(h + 1) * dx] * scale
        kh = kv_ref[:, h * dx:(h + 1) * dx]
        vh = kv_ref[:, d_x + h * dx:d_x + (h + 1) * dx]
        s = lax.dot_general(qh, kh, (((1,), (1,)), ((), ())), preferred_element_type=F32)
        m = jnp.max(s, axis=-1, keepdims=True)
        p = jnp.exp(s - m)
        l = jnp.sum(p, axis=-1, keepdims=True)
        o = jnp.dot(p.astype(BF16), vh, preferred_element_type=F32) / l
        z = z_ref[:, h * dx:(h + 1) * dx].astype(F32)
        o_ref[:, h * dx:(h + 1) * dx] = (o * _silu(z)).astype(o_ref.dtype)


def _cross_attention(proj, kv, B, S, n_mem, q_off, z_off):
    d_x = N_HEADS_X * HEAD_DIM_X
    TQ = XATT_TQ
    nq = S // TQ
    return pl.pallas_call(
        _xattn_kernel,
        grid=(B, nq),
        in_specs=[
            pl.BlockSpec((TQ, d_x), lambda b, i: (b * nq + i, q_off // d_x)),
            pl.BlockSpec((TQ, d_x), lambda b, i: (b * nq + i, z_off // d_x)),
            pl.BlockSpec((n_mem, 2 * d_x), lambda b, i: (b, 0)),
        ],
        out_specs=pl.BlockSpec((TQ, d_x), lambda b, i: (b * nq + i, 0)),
        out_shape=jax.ShapeDtypeStruct((B * S, d_x), BF16),
        compiler_params=_cparams(("parallel", "parallel")),
        name="mem_xattn",
    )(proj, proj, kv)


def _merge_kernel(a_ref, b_ref, c_ref, gt_ref, x_ref, woa_ref, wob_ref, woc_ref, wout_ref,
                  bm_ref, gp_ref, o_ref):
    D = x_ref.shape[1]
    merged = jnp.zeros(x_ref.shape, F32)
    for n, (br_ref, w_ref) in enumerate(((a_ref, woa_ref), (b_ref, wob_ref), (c_ref, woc_ref))):
        yb = jnp.dot(br_ref[...], w_ref[...], preferred_element_type=F32)
        gate = jax.nn.sigmoid(gt_ref[:, n * D:(n + 1) * D].astype(F32) + bm_ref[:, n * D:(n + 1) * D])
        merged = merged + gate * yb
    y = jnp.dot(merged.astype(BF16), wout_ref[...], preferred_element_type=F32)
    ms = jnp.mean(y * y, axis=-1, keepdims=True)
    o_ref[...] = x_ref[...] + y * lax.rsqrt(ms + EPS) * gp_ref[...]


def _merge(ya, yb, yc, proj, x2, w_oa, w_ob, w_oc, w_out, b_merge, g_post, g_off):
    T, D = x2.shape
    TM = MERGE_TM
    row = lambda i: (i, 0)
    fixed = lambda i: (0, 0)
    return pl.pallas_call(
        _merge_kernel,
        grid=(T // TM,),
        in_specs=[
            pl.BlockSpec((TM, D), row),
            pl.BlockSpec((TM, D), row),
            pl.BlockSpec((TM, D), row),
            pl.BlockSpec((TM, N_BRANCH * D), lambda i: (i, g_off // (N_BRANCH * D))),
            pl.BlockSpec((TM, D), row),
            pl.BlockSpec((D, D), fixed),
            pl.BlockSpec((D, D), fixed),
            pl.BlockSpec((D, D), fixed),
            pl.BlockSpec((D, D), fixed),
            pl.BlockSpec((1, N_BRANCH * D), fixed),
            pl.BlockSpec((1, D), fixed),
        ],
        out_specs=pl.BlockSpec((TM, D), row),
        out_shape=jax.ShapeDtypeStruct((T, D), F32),
        compiler_params=_cparams(("parallel",)),
        name="merge_out",
    )(ya, yb, yc, proj, x2, w_oa, w_ob, w_oc, w_out, b_merge, g_post)


def kernel(x, mem, rel_bias, g_pre, g_mem, w_in, b_merge, lam_q1, lam_k1, lam_q2, lam_k2,
           g_subln, w_oa, conv_w, conv_b, ln_g, ln_b, w_ob, w_mem_kv, w_oc, w_out, g_post):
    B, S, D = x.shape
    n_mem = mem.shape[1]
    depth = w_in.shape[0]
    d_a = N_HEADS_A * 2 * HEAD_DIM_A
    d_conv = conv_w.shape[2]
    d_x = N_HEADS_X * HEAD_DIM_X
    sizes = (d_a, d_a, d_a, d_a, 2 * d_conv, d_conv, d_x, d_x, N_BRANCH * D)
    offs = [int(o) for o in np.cumsum((0,) + sizes)]
    assert offs[-1] == w_in.shape[2] and d_a == d_conv == d_x == D
    assert MAX_DIST <= ATT_T and ATT_T % CHUNK == 0 and CONV_HALO >= CONV_K - 1

    x2 = x.reshape(B * S, D)
    mem2 = mem.reshape(B * n_mem, D)
    bias_tiles = _bias_tiles(rel_bias)
    for l in range(depth):
        lam_init = 0.8 - 0.6 * math.exp(-0.3 * l)
        proj = _proj(x2, g_pre[l][None], w_in[l].astype(BF16))

        lamv = jnp.stack([lam_q1[l], lam_k1[l], lam_q2[l], lam_k2[l]])
        ya = _diff_attention(proj, bias_tiles, lamv, g_subln[l][None], B, S, lam_init)

        yb = _conv_module(proj, conv_w[l], conv_b[l][None], ln_g[l][None], ln_b[l][None],
                          B, S, offs[4], offs[5])

        kv = _mem_kv(mem2, g_mem[l][None], w_mem_kv[l].astype(BF16), n_mem)
        yc = _cross_attention(proj, kv, B, S, n_mem, offs[6], offs[7])

        x2 = _merge(ya, yb, yc, proj, x2, w_oa[l].astype(BF16), w_ob[l].astype(BF16),
                    w_oc[l].astype(BF16), w_out[l].astype(BF16), b_merge[l][None], g_post[l][None],
                    offs[8])
    return x2.reshape(B, S, D)
```

```python
import functools
import math

import jax
import jax.numpy as jnp
import numpy as np
from jax import lax
from jax.experimental import pallas as pl
from jax.experimental.pallas import tpu as pltpu

CHUNK = 64
N_HEADS_A = 8
HEAD_DIM_A = 64
N_HEADS_X = 4
HEAD_DIM_X = 256
CONV_K = 31
N_BRANCH = 3
N_BUCKETS = 32
MAX_DIST = 128
EPS = 1e-6
NEG_INF = -1e30
LOG2_E = math.log2(math.e)
Q_SCALE_LOG2 = HEAD_DIM_A ** -0.5 * LOG2_E

LANES = 128
SUBLANES = 8
VMEM_LIMIT_BYTES = 48 * 1024 * 1024

PROJ_TM = 1024
PROJ_TN = 2048
ATT_T = 256
ATT_NH = 4
CONV_TS = 512
CONV_HALO = 32
CONV_RS = 32
XATT_TQ = 512
MERGE_TM = 512

BF16 = jnp.bfloat16
F32 = jnp.float32


def _cparams(sem):
    return pltpu.CompilerParams(dimension_semantics=sem, vmem_limit_bytes=VMEM_LIMIT_BYTES)


def _silu(z):
    return z * jax.nn.sigmoid(z)


def _proj_kernel(x_ref, g_ref, w_ref, cs_ref, o_ref, h_scr):
    @pl.when(pl.program_id(1) == 0)
    def _():
        xf = x_ref[...]
        ms = jnp.mean(xf * xf, axis=-1, keepdims=True)
        h_scr[...] = (xf * lax.rsqrt(ms + EPS) * g_ref[...]).astype(BF16)

    acc = jnp.dot(h_scr[...], w_ref[...], preferred_element_type=F32)
    tm, tn = acc.shape
    acc = acc.reshape(tm // SUBLANES, SUBLANES, tn) * cs_ref[...][None]
    o_ref[...] = acc.reshape(tm, tn).astype(o_ref.dtype)


def _proj(x2, g_pre, w_in, col_scale):
    T, D = x2.shape
    N = w_in.shape[1]
    return pl.pallas_call(
        _proj_kernel,
        grid=(T // PROJ_TM, N // PROJ_TN),
        in_specs=[
            pl.BlockSpec((PROJ_TM, D), lambda i, j: (i, 0)),
            pl.BlockSpec((1, D), lambda i, j: (0, 0)),
            pl.BlockSpec((D, PROJ_TN), lambda i, j: (0, j)),
            pl.BlockSpec((SUBLANES, PROJ_TN), lambda i, j: (0, j)),
        ],
        out_specs=pl.BlockSpec((PROJ_TM, PROJ_TN), lambda i, j: (i, j)),
        out_shape=jax.ShapeDtypeStruct((T, N), BF16),
        scratch_shapes=[pltpu.VMEM((PROJ_TM, D), BF16)],
        compiler_params=_cparams(("parallel", "arbitrary")),
        name="proj",
    )(x2, g_pre, w_in, col_scale)


def _t5_bucket(rel):
    nb = N_BUCKETS // 2
    max_exact = nb // 2
    ret = (rel > 0).astype(jnp.int32) * nb
    n = jnp.abs(rel)
    nf = jnp.maximum(n, 1).astype(jnp.float32)
    large = max_exact + (jnp.log(nf / max_exact) / math.log(MAX_DIST / max_exact)
                         * (nb - max_exact)).astype(jnp.int32)
    large = jnp.minimum(large, nb - 1)
    return ret + jnp.where(n < max_exact, n, large)


def _bucket_tiles():
    kpos = jnp.arange(ATT_T, dtype=jnp.int32)[:, None]
    qpos = jnp.arange(ATT_T, dtype=jnp.int32)[None, :]
    tiles = []
    for d in (0, -1, -2):
        rel = kpos + d * ATT_T - qpos
        tiles.append(_t5_bucket(rel))
    allowed = (kpos // CHUNK) <= (qpos // CHUNK)
    tiles[0] = jnp.where(allowed, tiles[0], -1)
    return jnp.stack(tiles)


def _bias_kernel(relb_ref, bkt_ref, o_ref):
    h = pl.program_id(0)
    bkt = bkt_ref[...]
    res = jnp.zeros(bkt.shape, F32)
    for b in range(N_BUCKETS):
        res = jnp.where(bkt == b, relb_ref[b, h], res)
    near = res[0:2] - res[2, 0:1, 0:1]
    o_ref[0] = jnp.where(bkt[0:2] < 0, NEG_INF, near * LOG2_E)


def _bias_tiles(rel_bias):
    bkt = _bucket_tiles()
    return pl.pallas_call(
        _bias_kernel,
        grid=(N_HEADS_A,),
        in_specs=[
            pl.BlockSpec(memory_space=pltpu.SMEM),
            pl.BlockSpec((3, ATT_T, ATT_T), lambda h: (0, 0, 0)),
        ],
        out_specs=pl.BlockSpec((1, 2, ATT_T, ATT_T), lambda h: (h, 0, 0, 0)),
        out_shape=jax.ShapeDtypeStruct((N_HEADS_A, 2, ATT_T, ATT_T), F32),
        compiler_params=_cparams(("arbitrary",)),
        name="bias_tiles",
    )(rel_bias, bkt)


def _attn_kernel(lamv_ref, gs_ref, q_ref, k_ref, v_ref, z_ref, bias_ref, o_ref,
                 qq_scr, sta_scr, stb_scr, acc_scr, m_scr, l_scr, *, lam_init):
    T = ATT_T
    qi = pl.program_id(2)

    lane = lax.broadcasted_iota(jnp.int32, (T, LANES), 1)
    for hh in range(ATT_NH):
        q = q_ref[:, hh * LANES:(hh + 1) * LANES]
        zero = jnp.zeros_like(q)
        qq_scr[hh, 0:T] = jnp.where(lane < HEAD_DIM_A, q, zero)
        qq_scr[hh, T:2 * T] = jnp.where(lane >= HEAD_DIM_A, q, zero)

    m_scr[...] = jnp.full(m_scr.shape, -jnp.inf, F32)
    l_scr[...] = jnp.zeros(l_scr.shape, F32)
    acc_scr[...] = jnp.zeros(acc_scr.shape, F32)
    ones = jnp.ones((SUBLANES, T), BF16)

    def scores(t, st_ref):
        k0 = pl.multiple_of(t * T, T)
        for hh in range(ATT_NH):
            kb = k_ref[pl.ds(k0, T), hh * LANES:(hh + 1) * LANES]
            st_ref[hh] = lax.dot_general(kb, qq_scr[hh], (((1,), (1,)), ((), ())),
                                         preferred_element_type=F32)

    def softmax_pv(t, st_ref, tile):
        k0 = pl.multiple_of(t * T, T)
        for hh in range(ATT_NH):
            st = st_ref[hh]
            if tile is not None:
                bias = bias_ref[hh, tile]
                st = st + jnp.concatenate([bias, bias], axis=1)
            m_prev = m_scr[hh]
            m_new = jnp.maximum(m_prev, jnp.max(st, axis=0, keepdims=True))
            alpha = jnp.exp2(m_prev - m_new)
            p = jnp.exp2(st - m_new).astype(BF16)
            vb = v_ref[pl.ds(k0, T), hh * LANES:(hh + 1) * LANES]
            l_scr[hh] = alpha * l_scr[hh] + jnp.dot(ones, p, preferred_element_type=F32)[0:1]
            pv = lax.dot_general(vb, p, (((0,), (0,)), ((), ())), preferred_element_type=F32)
            acc_scr[hh] = alpha * acc_scr[hh] + pv
            m_scr[hh] = m_new

    def tick(t, cur, nxt, tile):
        scores(t + 1, nxt)
        softmax_pv(t, cur, tile)

    n_far = jnp.maximum(qi - 1, 0)
    scores(0, sta_scr)

    def far_pair(i, carry):
        tick(2 * i, sta_scr, stb_scr, None)
        tick(2 * i + 1, stb_scr, sta_scr, None)
        return carry

    lax.fori_loop(0, n_far // 2, far_pair, 0)
    odd = lax.rem(n_far, 2) == 1

    @pl.when(odd)
    def _():
        tick(n_far - 1, sta_scr, stb_scr, None)

    @pl.when(jnp.logical_and(qi >= 1, jnp.logical_not(odd)))
    def _():
        tick(qi - 1, sta_scr, stb_scr, 1)
        softmax_pv(qi, stb_scr, 0)

    @pl.when(jnp.logical_and(qi >= 1, odd))
    def _():
        tick(qi - 1, stb_scr, sta_scr, 1)
        softmax_pv(qi, sta_scr, 0)

    @pl.when(qi == 0)
    def _():
        softmax_pv(0, sta_scr, 0)


    lv = lamv_ref[...]
    lam = (jnp.exp(jnp.sum(lv[0:1] * lv[1:2], axis=-1, keepdims=True))
           - jnp.exp(jnp.sum(lv[2:3] * lv[3:4], axis=-1, keepdims=True)) + lam_init)
    for hh in range(ATT_NH):
        ot = acc_scr[hh] / l_scr[hh]
        ot = ot[:, :T] - lam * ot[:, T:]
        ms = jnp.mean(ot * ot, axis=0, keepdims=True)
        on = (ot * lax.rsqrt(ms + EPS)).T * gs_ref[...] * (1.0 - lam_init)
        z = z_ref[:, hh * LANES:(hh + 1) * LANES].astype(F32)
        o_ref[:, hh * LANES:(hh + 1) * LANES] = (on * _silu(z)).astype(o_ref.dtype)


def _diff_attention(proj, bias_tiles, lamv, g_subln, B, S, lam_init):
    T = ATT_T
    nq = S // T
    H = N_HEADS_A
    d_a = H * 2 * HEAD_DIM_A
    W = ATT_NH * LANES
    G = H // ATT_NH
    return pl.pallas_call(
        functools.partial(_attn_kernel, lam_init=lam_init),
        grid=(B, G, nq),
        in_specs=[
            pl.BlockSpec((4, HEAD_DIM_A), lambda b, g, i: (0, 0)),
            pl.BlockSpec((1, LANES), lambda b, g, i: (0, 0)),
            pl.BlockSpec((T, W), lambda b, g, i: (b * nq + i, g)),
            pl.BlockSpec((S, W), lambda b, g, i: (b, G + g)),
            pl.BlockSpec((S, W), lambda b, g, i: (b, 2 * G + g)),
            pl.BlockSpec((T, W), lambda b, g, i: (b * nq + i, 3 * G + g)),
            pl.BlockSpec((ATT_NH, 2, T, T), lambda b, g, i: (g, 0, 0, 0)),
        ],
        out_specs=pl.BlockSpec((T, W), lambda b, g, i: (b * nq + i, g)),
        out_shape=jax.ShapeDtypeStruct((B * S, d_a), BF16),
        scratch_shapes=[pltpu.VMEM((ATT_NH, 2 * T, LANES), BF16),
                        pltpu.VMEM((ATT_NH, T, 2 * T), F32),
                        pltpu.VMEM((ATT_NH, T, 2 * T), F32),
                        pltpu.VMEM((ATT_NH, 2 * HEAD_DIM_A, 2 * T), F32),
                        pltpu.VMEM((ATT_NH, 1, 2 * T), F32),
                        pltpu.VMEM((ATT_NH, 1, 2 * T), F32)],
        compiler_params=_cparams(("parallel", "parallel", "arbitrary")),
        name="diff_attn",
    )(lamv, g_subln, proj, proj, proj, proj, bias_tiles)


def _conv_kernel(u_ref, halo_ref, z_ref, cw_ref, cb_ref, lng_ref, lnb_ref, o_ref, c_scr):
    C = o_ref.shape[1]
    TS = o_ref.shape[0]

    def glu(u):
        return u[:, :C].astype(F32) * jax.nn.sigmoid(u[:, C:].astype(F32))

    halo = glu(halo_ref[...])
    c_scr[0, 0:CONV_HALO] = jnp.where(pl.program_id(1) == 0, jnp.zeros_like(halo), halo)
    c_scr[0, CONV_HALO:CONV_HALO + TS] = glu(u_ref[...])
    c_scr[0, CONV_HALO + TS:] = jnp.zeros((SUBLANES, C), F32)

    def shift(r, carry):
        r0 = pl.multiple_of(r * CONV_RS, CONV_RS)
        win = c_scr[0, pl.ds(r0, CONV_RS + SUBLANES), :]
        for s in range(1, SUBLANES):
            c_scr[s, pl.ds(r0, CONV_RS), :] = win[s:s + CONV_RS]
        return carry

    lax.fori_loop(0, (CONV_HALO + TS) // CONV_RS, shift, 0)

    base = CONV_HALO - (CONV_K - 1)

    groups = CONV_RS // SUBLANES

    def chunk(r, carry):
        r0 = pl.multiple_of(r * CONV_RS, CONV_RS)
        accs = [cb_ref[...]] * groups
        for phase in range(SUBLANES):
            tiles = {}
            for a in range((base + CONV_K - 1) // SUBLANES + 1):
                j = a * SUBLANES + phase - base
                if not 0 <= j < CONV_K:
                    continue
                wj = cw_ref[j]
                for g in range(groups):
                    if g + a not in tiles:
                        row = r0 + (g + a) * SUBLANES
                        tiles[g + a] = c_scr[phase, pl.ds(row, SUBLANES), :]
                    accs[g] = accs[g] + wj * tiles[g + a]
        z = z_ref[pl.ds(r0, CONV_RS), :].astype(F32)
        ys = []
        for g in range(groups):
            mu = jnp.mean(accs[g], axis=-1, keepdims=True)
            xc = accs[g] - mu
            var = jnp.mean(xc * xc, axis=-1, keepdims=True)
            y = xc * lax.rsqrt(var + EPS) * lng_ref[...] + lnb_ref[...]
            ys.append(_silu(y) * _silu(z[g * SUBLANES:(g + 1) * SUBLANES]))
        o_ref[pl.ds(r0, CONV_RS), :] = jnp.concatenate(ys, axis=0).astype(o_ref.dtype)
        return carry

    lax.fori_loop(0, TS // CONV_RS, chunk, 0)


def _conv_module(proj, conv_w, conv_b, ln_g, ln_b, B, S, u_off, z_off):
    C = conv_w.shape[-1]
    TS = CONV_TS
    ns = S // TS
    hb = TS // CONV_HALO
    return pl.pallas_call(
        _conv_kernel,
        grid=(B, ns),
        in_specs=[
            pl.BlockSpec((TS, 2 * C), lambda b, s: (b * ns + s, u_off // (2 * C))),
            pl.BlockSpec((CONV_HALO, 2 * C),
                         lambda b, s: (jnp.maximum((b * ns + s) * hb - 1, 0), u_off // (2 * C))),
            pl.BlockSpec((TS, C), lambda b, s: (b * ns + s, z_off // C)),
            pl.BlockSpec((CONV_K, SUBLANES, C), lambda b, s: (0, 0, 0)),
            pl.BlockSpec((SUBLANES, C), lambda b, s: (0, 0)),
            pl.BlockSpec((SUBLANES, C), lambda b, s: (0, 0)),
            pl.BlockSpec((SUBLANES, C), lambda b, s: (0, 0)),
        ],
        out_specs=pl.BlockSpec((TS, C), lambda b, s: (b * ns + s, 0)),
        out_shape=jax.ShapeDtypeStruct((B * S, C), BF16),
        scratch_shapes=[pltpu.VMEM((SUBLANES, CONV_HALO + TS + SUBLANES, C), F32)],
        compiler_params=_cparams(("parallel", "parallel")),
        name="conv_module",
    )(proj, proj, proj, conv_w, conv_b, ln_g, ln_b)


def _memkv_kernel(mem_ref, g_ref, w_ref, o_ref):
    xf = mem_ref[...]
    ms = jnp.mean(xf * xf, axis=-1, keepdims=True)
    mn = (xf * lax.rsqrt(ms + EPS) * g_ref[...]).astype(BF16)
    o_ref[...] = jnp.dot(mn, w_ref[...], preferred_element_type=F32).astype(o_ref.dtype)


def _mem_kv(mem2, g_mem, w_mem_kv, n_mem):
    TM, D = mem2.shape
    N = w_mem_kv.shape[1]
    return pl.pallas_call(
        _memkv_kernel,
        grid=(TM // n_mem,),
        in_specs=[
            pl.BlockSpec((n_mem, D), lambda b: (b, 0)),
            pl.BlockSpec((1, D), lambda b: (0, 0)),
            pl.BlockSpec((D, N), lambda b: (0, 0)),
        ],
        out_specs=pl.BlockSpec((n_mem, N), lambda b: (b, 0)),
        out_shape=jax.ShapeDtypeStruct((TM, N), BF16),
        compiler_params=_cparams(("parallel",)),
        name="mem_kv",
    )(mem2, g_mem, w_mem_kv)


def _xattn_kernel(q_ref, z_ref, kv_ref, o_ref):
    dx = HEAD_DIM_X
    d_x = N_HEADS_X * dx
    scale = jnp.asarray(dx ** -0.5, BF16)
    for h in range(N_HEADS_X):
        qh = q_ref[:, h * dx:(h + 1) * dx] * scale
        kh = kv_ref[:, h * dx:(h + 1) * dx]
        vh = kv_ref[:, d_x + h * dx:d_x + (h + 1) * dx]
        s = lax.dot_general(qh, kh, (((1,), (1,)), ((), ())), preferred_element_type=F32)
        m = jnp.max(s, axis=-1, keepdims=True)
        p = jnp.exp(s - m)
        l = jnp.sum(p, axis=-1, keepdims=True)
        o = jnp.dot(p.astype(BF16), vh, preferred_element_type=F32) / l
        z = z_ref[:, h * dx:(h + 1) * dx].astype(F32)
        o_ref[:, h * dx:(h + 1) * dx] = (o * _silu(z)).astype(o_ref.dtype)


def _cross_attention(proj, kv, B, S, n_mem, q_off, z_off):
    d_x = N_HEADS_X * HEAD_DIM_X
    TQ = XATT_TQ
    nq = S // TQ
    return pl.pallas_call(
        _xattn_kernel,
        grid=(B, nq),
        in_specs=[
            pl.BlockSpec((TQ, d_x), lambda b, i: (b * nq + i, q_off // d_x)),
            pl.BlockSpec((TQ, d_x), lambda b, i: (b * nq + i, z_off // d_x)),
            pl.BlockSpec((n_mem, 2 * d_x), lambda b, i: (b, 0)),
        ],
        out_specs=pl.BlockSpec((TQ, d_x), lambda b, i: (b * nq + i, 0)),
        out_shape=jax.ShapeDtypeStruct((B * S, d_x), BF16),
        compiler_params=_cparams(("parallel", "parallel")),
        name="mem_xattn",
    )(proj, proj, kv)


def _merge_kernel(a_ref, b_ref, c_ref, gt_ref, x_ref, woa_ref, wob_ref, woc_ref, wout_ref,
                  bm_ref, gp_ref, o_ref):
    D = x_ref.shape[1]
    merged = jnp.zeros(x_ref.shape, F32)
    for n, (br_ref, w_ref) in enumerate(((a_ref, woa_ref), (b_ref, wob_ref), (c_ref, woc_ref))):
        yb = jnp.dot(br_ref[...], w_ref[...], preferred_element_type=F32)
        gate = jax.nn.sigmoid(gt_ref[:, n * D:(n + 1) * D].astype(F32) + bm_ref[:, n * D:(n + 1) * D])
        merged = merged + gate * yb
    y = jnp.dot(merged.astype(BF16), wout_ref[...], preferred_element_type=F32)
    ms = jnp.mean(y * y, axis=-1, keepdims=True)
    o_ref[...] = x_ref[...] + y * lax.rsqrt(ms + EPS) * gp_ref[...]


def _merge(ya, yb, yc, proj, x2, w_oa, w_ob, w_oc, w_out, b_merge, g_post, g_off):
    T, D = x2.shape
    TM = MERGE_TM
    row = lambda i: (i, 0)
    fixed = lambda i: (0, 0)
    return pl.pallas_call(
        _merge_kernel,
        grid=(T // TM,),
        in_specs=[
            pl.BlockSpec((TM, D), row),
            pl.BlockSpec((TM, D), row),
            pl.BlockSpec((TM, D), row),
            pl.BlockSpec((TM, N_BRANCH * D), lambda i: (i, g_off // (N_BRANCH * D))),
            pl.BlockSpec((TM, D), row),
            pl.BlockSpec((D, D), fixed),
            pl.BlockSpec((D, D), fixed),
            pl.BlockSpec((D, D), fixed),
            pl.BlockSpec((D, D), fixed),
            pl.BlockSpec((1, N_BRANCH * D), fixed),
            pl.BlockSpec((1, D), fixed),
        ],
        out_specs=pl.BlockSpec((TM, D), row),
        out_shape=jax.ShapeDtypeStruct((T, D), F32),
        compiler_params=_cparams(("parallel",)),
        name="merge_out",
    )(ya, yb, yc, proj, x2, w_oa, w_ob, w_oc, w_out, b_merge, g_post)


def kernel(x, mem, rel_bias, g_pre, g_mem, w_in, b_merge, lam_q1, lam_k1, lam_q2, lam_k2,
           g_subln, w_oa, conv_w, conv_b, ln_g, ln_b, w_ob, w_mem_kv, w_oc, w_out, g_post):
    B, S, D = x.shape
    n_mem = mem.shape[1]
    depth = w_in.shape[0]
    d_a = N_HEADS_A * 2 * HEAD_DIM_A
    d_conv = conv_w.shape[2]
    d_x = N_HEADS_X * HEAD_DIM_X
    sizes = (d_a, d_a, d_a, d_a, 2 * d_conv, d_conv, d_x, d_x, N_BRANCH * D)
    offs = [int(o) for o in np.cumsum((0,) + sizes)]
    assert offs[-1] == w_in.shape[2] and d_a == d_conv == d_x == D
    assert MAX_DIST <= ATT_T and ATT_T % CHUNK == 0 and CONV_HALO >= CONV_K - 1

    x2 = x.reshape(B * S, D)
    mem2 = mem.reshape(B * n_mem, D)
    bias_tiles = _bias_tiles(rel_bias)
    col_scale = jnp.ones((SUBLANES, offs[-1]), F32).at[:, :d_a].set(Q_SCALE_LOG2)
    for l in range(depth):
        lam_init = 0.8 - 0.6 * math.exp(-0.3 * l)
        proj = _proj(x2, g_pre[l][None], w_in[l].astype(BF16), col_scale)

        lamv = jnp.stack([lam_q1[l], lam_k1[l], lam_q2[l], lam_k2[l]])
        ya = _diff_attention(proj, bias_tiles, lamv, g_subln[l][None], B, S, lam_init)

        rep = lambda v: jnp.broadcast_to(v[..., None, :], v.shape[:-1] + (SUBLANES, v.shape[-1]))
        yb = _conv_module(proj, rep(conv_w[l]), rep(conv_b[l]), rep(ln_g[l]), rep(ln_b[l]),
                          B, S, offs[4], offs[5])

        kv = _mem_kv(mem2, g_mem[l][None], w_mem_kv[l].astype(BF16), n_mem)
        yc = _cross_attention(proj, kv, B, S, n_mem, offs[6], offs[7])

        x2 = _merge(ya, yb, yc, proj, x2, w_oa[l].astype(BF16), w_ob[l].astype(BF16),
                    w_oc[l].astype(BF16), w_out[l].astype(BF16), b_merge[l][None], g_post[l][None],
                    offs[8])
    return x2.reshape(B, S, D)
```

```python
import functools
import math

import jax
import jax.numpy as jnp
import numpy as np
from jax import lax
from jax.experimental import pallas as pl
from jax.experimental.pallas import tpu as pltpu

CHUNK = 64
N_HEADS_A = 8
HEAD_DIM_A = 64
N_HEADS_X = 4
HEAD_DIM_X = 256
CONV_K = 31
N_BRANCH = 3
N_BUCKETS = 32
MAX_DIST = 128
EPS = 1e-6
NEG_INF = -1e30
LOG2_E = math.log2(math.e)
Q_SCALE_LOG2 = HEAD_DIM_A ** -0.5 * LOG2_E

LANES = 128
SUBLANES = 8
VMEM_LIMIT_BYTES = 48 * 1024 * 1024

PROJ_TM = 1024
PROJ_TN = 2048
ATT_T = 256
ATT_NH = 8
CONV_TS = 512
CONV_HALO = 32
CONV_RS = 32
XATT_TQ = 512
MERGE_TM = 512

BF16 = jnp.bfloat16
F32 = jnp.float32


def _cparams(sem):
    return pltpu.CompilerParams(dimension_semantics=sem, vmem_limit_bytes=VMEM_LIMIT_BYTES)


def _silu(z):
    return z * jax.nn.sigmoid(z)


def _proj_kernel(x_ref, g_ref, w_ref, cs_ref, o_ref, h_scr):
    @pl.when(pl.program_id(1) == 0)
    def _():
        xf = x_ref[...]
        ms = jnp.mean(xf * xf, axis=-1, keepdims=True)
        h_scr[...] = (xf * lax.rsqrt(ms + EPS) * g_ref[...]).astype(BF16)

    acc = jnp.dot(h_scr[...], w_ref[...], preferred_element_type=F32)
    tm, tn = acc.shape
    acc = acc.reshape(tm // SUBLANES, SUBLANES, tn) * cs_ref[...][None]
    o_ref[...] = acc.reshape(tm, tn).astype(o_ref.dtype)


def _proj(x2, g_pre, w_in, col_scale):
    T, D = x2.shape
    N = w_in.shape[1]
    return pl.pallas_call(
        _proj_kernel,
        grid=(T // PROJ_TM, N // PROJ_TN),
        in_specs=[
            pl.BlockSpec((PROJ_TM, D), lambda i, j: (i, 0)),
            pl.BlockSpec((1, D), lambda i, j: (0, 0)),
            pl.BlockSpec((D, PROJ_TN), lambda i, j: (0, j)),
            pl.BlockSpec((SUBLANES, PROJ_TN), lambda i, j: (0, j)),
        ],
        out_specs=pl.BlockSpec((PROJ_TM, PROJ_TN), lambda i, j: (i, j)),
        out_shape=jax.ShapeDtypeStruct((T, N), BF16),
        scratch_shapes=[pltpu.VMEM((PROJ_TM, D), BF16)],
        compiler_params=_cparams(("parallel", "arbitrary")),
        name="proj",
    )(x2, g_pre, w_in, col_scale)


def _t5_bucket(rel):
    nb = N_BUCKETS // 2
    max_exact = nb // 2
    ret = (rel > 0).astype(jnp.int32) * nb
    n = jnp.abs(rel)
    nf = jnp.maximum(n, 1).astype(jnp.float32)
    large = max_exact + (jnp.log(nf / max_exact) / math.log(MAX_DIST / max_exact)
                         * (nb - max_exact)).astype(jnp.int32)
    large = jnp.minimum(large, nb - 1)
    return ret + jnp.where(n < max_exact, n, large)


def _bucket_tiles():
    kpos = jnp.arange(ATT_T, dtype=jnp.int32)[:, None]
    qpos = jnp.arange(ATT_T, dtype=jnp.int32)[None, :]
    tiles = []
    for d in (0, -1, -2):
        rel = kpos + d * ATT_T - qpos
        tiles.append(_t5_bucket(rel))
    allowed = (kpos // CHUNK) <= (qpos // CHUNK)
    tiles[0] = jnp.where(allowed, tiles[0], -1)
    return jnp.stack(tiles)


def _bias_kernel(relb_ref, bkt_ref, o_ref):
    h = pl.program_id(0)
    bkt = bkt_ref[...]
    res = jnp.zeros(bkt.shape, F32)
    for b in range(N_BUCKETS):
        res = jnp.where(bkt == b, relb_ref[b, h], res)
    near = res[0:2] - res[2, 0:1, 0:1]
    o_ref[0] = jnp.where(bkt[0:2] < 0, NEG_INF, near * LOG2_E)


def _bias_tiles(rel_bias):
    bkt = _bucket_tiles()
    return pl.pallas_call(
        _bias_kernel,
        grid=(N_HEADS_A,),
        in_specs=[
            pl.BlockSpec(memory_space=pltpu.SMEM),
            pl.BlockSpec((3, ATT_T, ATT_T), lambda h: (0, 0, 0)),
        ],
        out_specs=pl.BlockSpec((1, 2, ATT_T, ATT_T), lambda h: (h, 0, 0, 0)),
        out_shape=jax.ShapeDtypeStruct((N_HEADS_A, 2, ATT_T, ATT_T), F32),
        compiler_params=_cparams(("arbitrary",)),
        name="bias_tiles",
    )(rel_bias, bkt)


def _attn_kernel(lamv_ref, gs_ref, q_ref, qn_ref, k_ref, v_ref, z_ref, bias_ref, o_ref,
                 qq_scr, sta_scr, stb_scr, acc_scr, m_scr, l_scr, *, lam_init):
    T = ATT_T
    qi = pl.program_id(2)
    ones = jnp.ones((SUBLANES, T), BF16)

    def scores(t, st_ref):
        k0 = pl.multiple_of(t * T, T)
        for hh in range(ATT_NH):
            kb = k_ref[pl.ds(k0, T), hh * LANES:(hh + 1) * LANES]
            st_ref[hh] = lax.dot_general(kb, qq_scr[hh], (((1,), (1,)), ((), ())),
                                         preferred_element_type=F32)

    def start_block(src_ref):
        lane = lax.broadcasted_iota(jnp.int32, (T, LANES), 1)
        for hh in range(ATT_NH):
            q = src_ref[:, hh * LANES:(hh + 1) * LANES]
            zero = jnp.zeros_like(q)
            qq_scr[hh, 0:T] = jnp.where(lane < HEAD_DIM_A, q, zero)
            qq_scr[hh, T:2 * T] = jnp.where(lane >= HEAD_DIM_A, q, zero)
        m_scr[...] = jnp.full(m_scr.shape, -jnp.inf, F32)
        l_scr[...] = jnp.zeros(l_scr.shape, F32)
        acc_scr[...] = jnp.zeros(acc_scr.shape, F32)
        scores(0, sta_scr)

    @pl.when(qi == 0)
    def _():
        start_block(q_ref)

    def softmax_pv(t, st_ref, tile):
        k0 = pl.multiple_of(t * T, T)
        for hh in range(ATT_NH):
            st = st_ref[hh]
            if tile is not None:
                bias = bias_ref[hh, tile]
                st = st + jnp.concatenate([bias, bias], axis=1)
            m_prev = m_scr[hh]
            m_new = jnp.maximum(m_prev, jnp.max(st, axis=0, keepdims=True))
            alpha = jnp.exp2(m_prev - m_new)
            p = jnp.exp2(st - m_new).astype(BF16)
            vb = v_ref[pl.ds(k0, T), hh * LANES:(hh + 1) * LANES]
            l_scr[hh] = alpha * l_scr[hh] + jnp.dot(ones, p, preferred_element_type=F32)[0:1]
            pv = lax.dot_general(vb, p, (((0,), (0,)), ((), ())), preferred_element_type=F32)
            acc_scr[hh] = alpha * acc_scr[hh] + pv
            m_scr[hh] = m_new

    def tick(t, cur, nxt, tile):
        scores(t + 1, nxt)
        softmax_pv(t, cur, tile)

    n_far = jnp.maximum(qi - 1, 0)

    def far_pair(i, carry):
        tick(2 * i, sta_scr, stb_scr, None)
        tick(2 * i + 1, stb_scr, sta_scr, None)
        return carry

    lax.fori_loop(0, n_far // 2, far_pair, 0)
    odd = lax.rem(n_far, 2) == 1

    @pl.when(odd)
    def _():
        tick(n_far - 1, sta_scr, stb_scr, None)

    @pl.when(jnp.logical_and(qi >= 1, jnp.logical_not(odd)))
    def _():
        tick(qi - 1, sta_scr, stb_scr, 1)
        softmax_pv(qi, stb_scr, 0)

    @pl.when(jnp.logical_and(qi >= 1, odd))
    def _():
        tick(qi - 1, stb_scr, sta_scr, 1)
        softmax_pv(qi, sta_scr, 0)

    @pl.when(qi == 0)
    def _():
        softmax_pv(0, sta_scr, 0)

    lv = lamv_ref[...]
    lam = (jnp.exp(jnp.sum(lv[0:1] * lv[1:2], axis=-1, keepdims=True))
           - jnp.exp(jnp.sum(lv[2:3] * lv[3:4], axis=-1, keepdims=True)) + lam_init)
    for hh in range(ATT_NH):
        ot = acc_scr[hh] / l_scr[hh]
        ot = ot[:, :T] - lam * ot[:, T:]
        ms = jnp.mean(ot * ot, axis=0, keepdims=True)
        on = (ot * lax.rsqrt(ms + EPS)).T * gs_ref[...] * (1.0 - lam_init)
        z = z_ref[:, hh * LANES:(hh + 1) * LANES].astype(F32)
        o_ref[:, hh * LANES:(hh + 1) * LANES] = (on * _silu(z)).astype(o_ref.dtype)

    start_block(qn_ref)


def _diff_attention(proj, bias_tiles, lamv, g_subln, B, S, lam_init):
    T = ATT_T
    nq = S // T
    H = N_HEADS_A
    d_a = H * 2 * HEAD_DIM_A
    W = ATT_NH * LANES
    G = H // ATT_NH
    return pl.pallas_call(
        functools.partial(_attn_kernel, lam_init=lam_init),
        grid=(B, G, nq),
        in_specs=[
            pl.BlockSpec((4, HEAD_DIM_A), lambda b, g, i: (0, 0)),
            pl.BlockSpec((1, LANES), lambda b, g, i: (0, 0)),
            pl.BlockSpec((T, W), lambda b, g, i: (b * nq + i, g)),
            pl.BlockSpec((T, W), lambda b, g, i: (b * nq + jnp.minimum(i + 1, nq - 1), g)),
            pl.BlockSpec((S, W), lambda b, g, i: (b, G + g)),
            pl.BlockSpec((S, W), lambda b, g, i: (b, 2 * G + g)),
            pl.BlockSpec((T, W), lambda b, g, i: (b * nq + i, 3 * G + g)),
            pl.BlockSpec((ATT_NH, 2, T, T), lambda b, g, i: (g, 0, 0, 0)),
        ],
        out_specs=pl.BlockSpec((T, W), lambda b, g, i: (b * nq + i, g)),
        out_shape=jax.ShapeDtypeStruct((B * S, d_a), BF16),
        scratch_shapes=[pltpu.VMEM((ATT_NH, 2 * T, LANES), BF16),
                        pltpu.VMEM((ATT_NH, T, 2 * T), F32),
                        pltpu.VMEM((ATT_NH, T, 2 * T), F32),
                        pltpu.VMEM((ATT_NH, 2 * HEAD_DIM_A, 2 * T), F32),
                        pltpu.VMEM((ATT_NH, 1, 2 * T), F32),
                        pltpu.VMEM((ATT_NH, 1, 2 * T), F32)],
        compiler_params=_cparams(("parallel", "parallel", "arbitrary")),
        name="diff_attn",
    )(lamv, g_subln, proj, proj, proj, proj, proj, bias_tiles)


def _conv_kernel(u_ref, halo_ref, z_ref, cw_ref, cb_ref, lng_ref, lnb_ref, o_ref, c_scr):
    C = o_ref.shape[1]
    TS = o_ref.shape[0]

    def glu(u):
        return u[:, :C].astype(F32) * jax.nn.sigmoid(u[:, C:].astype(F32))

    halo = glu(halo_ref[...])
    c_scr[0, 0:CONV_HALO] = jnp.where(pl.program_id(1) == 0, jnp.zeros_like(halo), halo)
    c_scr[0, CONV_HALO:CONV_HALO + TS] = glu(u_ref[...])
    c_scr[0, CONV_HALO + TS:] = jnp.zeros((SUBLANES, C), F32)

    def shift(r, carry):
        r0 = pl.multiple_of(r * CONV_RS, CONV_RS)
        win = c_scr[0, pl.ds(r0, CONV_RS + SUBLANES), :]
        for s in range(1, SUBLANES):
            c_scr[s, pl.ds(r0, CONV_RS), :] = win[s:s + CONV_RS]
        return carry

    lax.fori_loop(0, (CONV_HALO + TS) // CONV_RS, shift, 0)

    base = CONV_HALO - (CONV_K - 1)

    groups = CONV_RS // SUBLANES

    def chunk(r, carry):
        r0 = pl.multiple_of(r * CONV_RS, CONV_RS)
        accs = [cb_ref[...]] * groups
        for phase in range(SUBLANES):
            tiles = {}
            for a in range((base + CONV_K - 1) // SUBLANES + 1):
                j = a * SUBLANES + phase - base
                if not 0 <= j < CONV_K:
                    continue
                wj = cw_ref[j]
                for g in range(groups):
                    if g + a not in tiles:
                        row = r0 + (g + a) * SUBLANES
                        tiles[g + a] = c_scr[phase, pl.ds(row, SUBLANES), :]
                    accs[g] = accs[g] + wj * tiles[g + a]
        z = z_ref[pl.ds(r0, CONV_RS), :].astype(F32)
        ys = []
        for g in range(groups):
            mu = jnp.mean(accs[g], axis=-1, keepdims=True)
            xc = accs[g] - mu
            var = jnp.mean(xc * xc, axis=-1, keepdims=True)
            y = xc * lax.rsqrt(var + EPS) * lng_ref[...] + lnb_ref[...]
            ys.append(_silu(y) * _silu(z[g * SUBLANES:(g + 1) * SUBLANES]))
        o_ref[pl.ds(r0, CONV_RS), :] = jnp.concatenate(ys, axis=0).astype(o_ref.dtype)
        return carry

    lax.fori_loop(0, TS // CONV_RS, chunk, 0)


def _conv_module(proj, conv_w, conv_b, ln_g, ln_b, B, S, u_off, z_off):
    C = conv_w.shape[-1]
    TS = CONV_TS
    ns = S // TS
    hb = TS // CONV_HALO
    return pl.pallas_call(
        _conv_kernel,
        grid=(B, ns),
        in_specs=[
            pl.BlockSpec((TS, 2 * C), lambda b, s: (b * ns + s, u_off // (2 * C))),
            pl.BlockSpec((CONV_HALO, 2 * C),
                         lambda b, s: (jnp.maximum((b * ns + s) * hb - 1, 0), u_off // (2 * C))),
            pl.BlockSpec((TS, C), lambda b, s: (b * ns + s, z_off // C)),
            pl.BlockSpec((CONV_K, SUBLANES, C), lambda b, s: (0, 0, 0)),
            pl.BlockSpec((SUBLANES, C), lambda b, s: (0, 0)),
            pl.BlockSpec((SUBLANES, C), lambda b, s: (0, 0)),
            pl.BlockSpec((SUBLANES, C), lambda b, s: (0, 0)),
        ],
        out_specs=pl.BlockSpec((TS, C), lambda b, s: (b * ns + s, 0)),
        out_shape=jax.ShapeDtypeStruct((B * S, C), BF16),
        scratch_shapes=[pltpu.VMEM((SUBLANES, CONV_HALO + TS + SUBLANES, C), F32)],
        compiler_params=_cparams(("parallel", "parallel")),
        name="conv_module",
    )(proj, proj, proj, conv_w, conv_b, ln_g, ln_b)


def _memkv_kernel(mem_ref, g_ref, w_ref, o_ref):
    xf = mem_ref[...]
    ms = jnp.mean(xf * xf, axis=-1, keepdims=True)
    mn = (xf * lax.rsqrt(ms + EPS) * g_ref[...]).astype(BF16)
    o_ref[...] = jnp.dot(mn, w_ref[...], preferred_element_type=F32).astype(o_ref.dtype)


def _mem_kv(mem2, g_mem, w_mem_kv, n_mem):
    TM, D = mem2.shape
    N = w_mem_kv.shape[1]
    return pl.pallas_call(
        _memkv_kernel,
        grid=(TM // n_mem,),
        in_specs=[
            pl.BlockSpec((n_mem, D), lambda b: (b, 0)),
            pl.BlockSpec((1, D), lambda b: (0, 0)),
            pl.BlockSpec((D, N), lambda b: (0, 0)),
        ],
        out_specs=pl.BlockSpec((n_mem, N), lambda b: (b, 0)),
        out_shape=jax.ShapeDtypeStruct((TM, N), BF16),
        compiler_params=_cparams(("parallel",)),
        name="mem_kv",
    )(mem2, g_mem, w_mem_kv)


def _xattn_kernel(q_ref, z_ref, kv_ref, o_ref):
    dx = HEAD_DIM_X
    d_x = N_HEADS_X * dx
    scale = jnp.asarray(dx ** -0.5, BF16)
    for h in range(N_HEADS_X):
        qh = q_ref[:, h * dx:(h + 1) * dx] * scale
        kh = kv_ref[:, h * dx:(h + 1) * dx]
        vh = kv_ref[:, d_x + h * dx:d_x + (h + 1) * dx]
        s = lax.dot_general(qh, kh, (((1,), (1,)), ((), ())), preferred_element_type=F32)
        m = jnp.max(s, axis=-1, keepdims=True)
        p = jnp.exp(s - m)
        l = jnp.sum(p, axis=-1, keepdims=True)
        o = jnp.dot(p.astype(BF16), vh, preferred_element_type=F32) / l
        z = z_ref[:, h * dx:(h + 1) * dx].astype(F32)
        o_ref[:, h * dx:(h + 1) * dx] = (o * _silu(z)).astype(o_ref.dtype)


def _cross_attention(proj, kv, B, S, n_mem, q_off, z_off):
    d_x = N_HEADS_X * HEAD_DIM_X
    TQ = XATT_TQ
    nq = S // TQ
    return pl.pallas_call(
        _xattn_kernel,
        grid=(B, nq),
        in_specs=[
            pl.BlockSpec((TQ, d_x), lambda b, i: (b * nq + i, q_off // d_x)),
            pl.BlockSpec((TQ, d_x), lambda b, i: (b * nq + i, z_off // d_x)),
            pl.BlockSpec((n_mem, 2 * d_x), lambda b, i: (b, 0)),
        ],
        out_specs=pl.BlockSpec((TQ, d_x), lambda b, i: (b * nq + i, 0)),
        out_shape=jax.ShapeDtypeStruct((B * S, d_x), BF16),
        compiler_params=_cparams(("parallel", "parallel")),
        name="mem_xattn",
    )(proj, proj, kv)


def _merge_kernel(a_ref, b_ref, c_ref, gt_ref, x_ref, woa_ref, wob_ref, woc_ref, wout_ref,
                  bm_ref, gp_ref, o_ref):
    D = x_ref.shape[1]
    merged = jnp.zeros(x_ref.shape, F32)
    for n, (br_ref, w_ref) in enumerate(((a_ref, woa_ref), (b_ref, wob_ref), (c_ref, woc_ref))):
        yb = jnp.dot(br_ref[...], w_ref[...], preferred_element_type=F32)
        gate = jax.nn.sigmoid(gt_ref[:, n * D:(n + 1) * D].astype(F32) + bm_ref[:, n * D:(n + 1) * D])
        merged = merged + gate * yb
    y = jnp.dot(merged.astype(BF16), wout_ref[...], preferred_element_type=F32)
    ms = jnp.mean(y * y, axis=-1, keepdims=True)
    o_ref[...] = x_ref[...] + y * lax.rsqrt(ms + EPS) * gp_ref[...]


def _merge(ya, yb, yc, proj, x2, w_oa, w_ob, w_oc, w_out, b_merge, g_post, g_off):
    T, D = x2.shape
    TM = MERGE_TM
    row = lambda i: (i, 0)
    fixed = lambda i: (0, 0)
    return pl.pallas_call(
        _merge_kernel,
        grid=(T // TM,),
        in_specs=[
            pl.BlockSpec((TM, D), row),
            pl.BlockSpec((TM, D), row),
            pl.BlockSpec((TM, D), row),
            pl.BlockSpec((TM, N_BRANCH * D), lambda i: (i, g_off // (N_BRANCH * D))),
            pl.BlockSpec((TM, D), row),
            pl.BlockSpec((D, D), fixed),
            pl.BlockSpec((D, D), fixed),
            pl.BlockSpec((D, D), fixed),
            pl.BlockSpec((D, D), fixed),
            pl.BlockSpec((1, N_BRANCH * D), fixed),
            pl.BlockSpec((1, D), fixed),
        ],
        out_specs=pl.BlockSpec((TM, D), row),
        out_shape=jax.ShapeDtypeStruct((T, D), F32),
        compiler_params=_cparams(("parallel",)),
        name="merge_out",
    )(ya, yb, yc, proj, x2, w_oa, w_ob, w_oc, w_out, b_merge, g_post)


def kernel(x, mem, rel_bias, g_pre, g_mem, w_in, b_merge, lam_q1, lam_k1, lam_q2, lam_k2,
           g_subln, w_oa, conv_w, conv_b, ln_g, ln_b, w_ob, w_mem_kv, w_oc, w_out, g_post):
    B, S, D = x.shape
    n_mem = mem.shape[1]
    depth = w_in.shape[0]
    d_a = N_HEADS_A * 2 * HEAD_DIM_A
    d_conv = conv_w.shape[2]
    d_x = N_HEADS_X * HEAD_DIM_X
    sizes = (d_a, d_a, d_a, d_a, 2 * d_conv, d_conv, d_x, d_x, N_BRANCH * D)
    offs = [int(o) for o in np.cumsum((0,) + sizes)]
    assert offs[-1] == w_in.shape[2] and d_a == d_conv == d_x == D
    assert MAX_DIST <= ATT_T and ATT_T % CHUNK == 0 and CONV_HALO >= CONV_K - 1

    x2 = x.reshape(B * S, D)
    mem2 = mem.reshape(B * n_mem, D)
    bias_tiles = _bias_tiles(rel_bias)
    col_scale = jnp.ones((SUBLANES, offs[-1]), F32).at[:, :d_a].set(Q_SCALE_LOG2)
    for l in range(depth):
        lam_init = 0.8 - 0.6 * math.exp(-0.3 * l)
        proj = _proj(x2, g_pre[l][None], w_in[l].astype(BF16), col_scale)

        lamv = jnp.stack([lam_q1[l], lam_k1[l], lam_q2[l], lam_k2[l]])
        ya = _diff_attention(proj, bias_tiles, lamv, g_subln[l][None], B, S, lam_init)

        rep = lambda v: jnp.broadcast_to(v[..., None, :], v.shape[:-1] + (SUBLANES, v.shape[-1]))
        yb = _conv_module(proj, rep(conv_w[l]), rep(conv_b[l]), rep(ln_g[l]), rep(ln_b[l]),
                          B, S, offs[4], offs[5])

        kv = _mem_kv(mem2, g_mem[l][None], w_mem_kv[l].astype(BF16), n_mem)
        yc = _cross_attention(proj, kv, B, S, n_mem, offs[6], offs[7])

        x2 = _merge(ya, yb, yc, proj, x2, w_oa[l].astype(BF16), w_ob[l].astype(BF16),
                    w_oc[l].astype(BF16), w_out[l].astype(BF16), b_merge[l][None], g_post[l][None],
                    offs[8])
    return x2.reshape(B, S, D)
```

```python
import functools
import math

import jax
import jax.numpy as jnp
import numpy as np
from jax import lax
from jax.experimental import pallas as pl
from jax.experimental.pallas import tpu as pltpu

CHUNK = 64
N_HEADS_A = 8
HEAD_DIM_A = 64
N_HEADS_X = 4
HEAD_DIM_X = 256
CONV_K = 31
N_BRANCH = 3
N_BUCKETS = 32
MAX_DIST = 128
EPS = 1e-6
NEG_INF = -1e30
LOG2_E = math.log2(math.e)
Q_SCALE_LOG2 = HEAD_DIM_A ** -0.5 * LOG2_E

LANES = 128
SUBLANES = 8
VMEM_LIMIT_BYTES = 48 * 1024 * 1024

PROJ_TM = 1024
PROJ_TN = 4096
ATT_T = 256
ATT_NH = 8
ATT_SKEW = 1
CONV_TS = 512
CONV_HALO = 32
CONV_RS = 32
XATT_TQ = 512
MERGE_TM = 512

BF16 = jnp.bfloat16
F32 = jnp.float32


def _cparams(sem):
    return pltpu.CompilerParams(dimension_semantics=sem, vmem_limit_bytes=VMEM_LIMIT_BYTES)


def _silu(z):
    return z * jax.nn.sigmoid(z)


def _proj_kernel(x_ref, g_ref, w_ref, cs_ref, o_ref, h_scr):
    @pl.when(pl.program_id(1) == 0)
    def _():
        xf = x_ref[...]
        ms = jnp.mean(xf * xf, axis=-1, keepdims=True)
        h_scr[...] = (xf * lax.rsqrt(ms + EPS) * g_ref[...]).astype(BF16)

    acc = jnp.dot(h_scr[...], w_ref[...], preferred_element_type=F32)
    tm, tn = acc.shape
    acc = acc.reshape(tm // SUBLANES, SUBLANES, tn) * cs_ref[...][None]
    o_ref[...] = acc.reshape(tm, tn).astype(o_ref.dtype)


def _proj(x2, g_pre, w_in, col_scale):
    T, D = x2.shape
    N = w_in.shape[1]
    return pl.pallas_call(
        _proj_kernel,
        grid=(T // PROJ_TM, N // PROJ_TN),
        in_specs=[
            pl.BlockSpec((PROJ_TM, D), lambda i, j: (i, 0)),
            pl.BlockSpec((1, D), lambda i, j: (0, 0)),
            pl.BlockSpec((D, PROJ_TN), lambda i, j: (0, j)),
            pl.BlockSpec((SUBLANES, PROJ_TN), lambda i, j: (0, j)),
        ],
        out_specs=pl.BlockSpec((PROJ_TM, PROJ_TN), lambda i, j: (i, j)),
        out_shape=jax.ShapeDtypeStruct((T, N), BF16),
        scratch_shapes=[pltpu.VMEM((PROJ_TM, D), BF16)],
        compiler_params=_cparams(("parallel", "arbitrary")),
        name="proj",
    )(x2, g_pre, w_in, col_scale)


def _t5_bucket(rel):
    nb = N_BUCKETS // 2
    max_exact = nb // 2
    ret = (rel > 0).astype(jnp.int32) * nb
    n = jnp.abs(rel)
    nf = jnp.maximum(n, 1).astype(jnp.float32)
    large = max_exact + (jnp.log(nf / max_exact) / math.log(MAX_DIST / max_exact)
                         * (nb - max_exact)).astype(jnp.int32)
    large = jnp.minimum(large, nb - 1)
    return ret + jnp.where(n < max_exact, n, large)


def _bucket_tiles():
    kpos = jnp.arange(ATT_T, dtype=jnp.int32)[:, None]
    qpos = jnp.arange(ATT_T, dtype=jnp.int32)[None, :]
    tiles = []
    for d in (0, -1, -2):
        rel = kpos + d * ATT_T - qpos
        tiles.append(_t5_bucket(rel))
    allowed = (kpos // CHUNK) <= (qpos // CHUNK)
    tiles[0] = jnp.where(allowed, tiles[0], -1)
    return jnp.stack(tiles)


def _bias_kernel(relb_ref, bkt_ref, o_ref):
    h = pl.program_id(0)
    bkt = bkt_ref[...]
    res = jnp.zeros(bkt.shape, F32)
    for b in range(N_BUCKETS):
        res = jnp.where(bkt == b, relb_ref[b, h], res)
    near = res[0:2] - res[2, 0:1, 0:1]
    o_ref[0] = jnp.where(bkt[0:2] < 0, NEG_INF, near * LOG2_E)


def _bias_tiles(rel_bias):
    bkt = _bucket_tiles()
    return pl.pallas_call(
        _bias_kernel,
        grid=(N_HEADS_A,),
        in_specs=[
            pl.BlockSpec(memory_space=pltpu.SMEM),
            pl.BlockSpec((3, ATT_T, ATT_T), lambda h: (0, 0, 0)),
        ],
        out_specs=pl.BlockSpec((1, 2, ATT_T, ATT_T), lambda h: (h, 0, 0, 0)),
        out_shape=jax.ShapeDtypeStruct((N_HEADS_A, 2, ATT_T, ATT_T), F32),
        compiler_params=_cparams(("arbitrary",)),
        name="bias_tiles",
    )(rel_bias, bkt)


def _attn_kernel(lamv_ref, gs_ref, q_ref, qn_ref, k_ref, v_ref, z_ref, bias_ref, o_ref,
                 qq_scr, sta_scr, stb_scr, acc_scr, m_scr, l_scr, *, lam_init):
    T = ATT_T
    qi = pl.program_id(2)
    ones = jnp.ones((SUBLANES, T), BF16)

    all_heads = tuple(range(ATT_NH))

    def scores(t, st_ref, heads=all_heads):
        k0 = pl.multiple_of(t * T, T)
        for hh in heads:
            kb = k_ref[pl.ds(k0, T), hh * LANES:(hh + 1) * LANES]
            st_ref[hh] = lax.dot_general(kb, qq_scr[hh], (((1,), (1,)), ((), ())),
                                         preferred_element_type=F32)

    def start_block(src_ref):
        lane = lax.broadcasted_iota(jnp.int32, (T, LANES), 1)
        for hh in range(ATT_NH):
            q = src_ref[:, hh * LANES:(hh + 1) * LANES]
            zero = jnp.zeros_like(q)
            qq_scr[hh, 0:T] = jnp.where(lane < HEAD_DIM_A, q, zero)
            qq_scr[hh, T:2 * T] = jnp.where(lane >= HEAD_DIM_A, q, zero)
        m_scr[...] = jnp.full(m_scr.shape, -jnp.inf, F32)
        l_scr[...] = jnp.zeros(l_scr.shape, F32)
        acc_scr[...] = jnp.zeros(acc_scr.shape, F32)
        scores(0, sta_scr)

    @pl.when(qi == 0)
    def _():
        start_block(q_ref)

    def softmax_pv(t, st_ref, tile, heads=all_heads):
        k0 = pl.multiple_of(t * T, T)
        for hh in heads:
            st = st_ref[hh]
            if tile is not None:
                bias = bias_ref[hh, tile]
                st = st + jnp.concatenate([bias, bias], axis=1)
            m_prev = m_scr[hh]
            m_new = jnp.maximum(m_prev, jnp.max(st, axis=0, keepdims=True))
            alpha = jnp.exp2(m_prev - m_new)
            p = jnp.exp2(st - m_new).astype(BF16)
            vb = v_ref[pl.ds(k0, T), hh * LANES:(hh + 1) * LANES]
            l_scr[hh] = alpha * l_scr[hh] + jnp.dot(ones, p, preferred_element_type=F32)[0:1]
            pv = lax.dot_general(vb, p, (((0,), (0,)), ((), ())), preferred_element_type=F32)
            acc_scr[hh] = alpha * acc_scr[hh] + pv
            m_scr[hh] = m_new

    def tick(t, cur, nxt, tile):
        scores(t + 1, nxt, all_heads[:ATT_SKEW])
        for hh in all_heads:
            scores(t + 1, nxt, all_heads[hh + ATT_SKEW:hh + ATT_SKEW + 1])
            softmax_pv(t, cur, tile, (hh,))

    n_far = jnp.maximum(qi - 1, 0)

    def far_pair(i, carry):
        tick(2 * i, sta_scr, stb_scr, None)
        tick(2 * i + 1, stb_scr, sta_scr, None)
        return carry

    lax.fori_loop(0, n_far // 2, far_pair, 0)
    odd = lax.rem(n_far, 2) == 1

    @pl.when(odd)
    def _():
        tick(n_far - 1, sta_scr, stb_scr, None)

    @pl.when(jnp.logical_and(qi >= 1, jnp.logical_not(odd)))
    def _():
        tick(qi - 1, sta_scr, stb_scr, 1)
        softmax_pv(qi, stb_scr, 0)

    @pl.when(jnp.logical_and(qi >= 1, odd))
    def _():
        tick(qi - 1, stb_scr, sta_scr, 1)
        softmax_pv(qi, sta_scr, 0)

    @pl.when(qi == 0)
    def _():
        softmax_pv(0, sta_scr, 0)

    lv = lamv_ref[...]
    lam = (jnp.exp(jnp.sum(lv[0:1] * lv[1:2], axis=-1, keepdims=True))
           - jnp.exp(jnp.sum(lv[2:3] * lv[3:4], axis=-1, keepdims=True)) + lam_init)
    for hh in range(ATT_NH):
        ot = acc_scr[hh] / l_scr[hh]
        ot = ot[:, :T] - lam * ot[:, T:]
        ms = jnp.mean(ot * ot, axis=0, keepdims=True)
        on = (ot * lax.rsqrt(ms + EPS)).T * gs_ref[...] * (1.0 - lam_init)
        z = z_ref[:, hh * LANES:(hh + 1) * LANES].astype(F32)
        o_ref[:, hh * LANES:(hh + 1) * LANES] = (on * _silu(z)).astype(o_ref.dtype)

    start_block(qn_ref)


def _diff_attention(proj, bias_tiles, lamv, g_subln, B, S, lam_init):
    T = ATT_T
    nq = S // T
    H = N_HEADS_A
    d_a = H * 2 * HEAD_DIM_A
    W = ATT_NH * LANES
    G = H // ATT_NH
    return pl.pallas_call(
        functools.partial(_attn_kernel, lam_init=lam_init),
        grid=(B, G, nq),
        in_specs=[
            pl.BlockSpec((4, HEAD_DIM_A), lambda b, g, i: (0, 0)),
            pl.BlockSpec((1, LANES), lambda b, g, i: (0, 0)),
            pl.BlockSpec((T, W), lambda b, g, i: (b * nq + i, g)),
            pl.BlockSpec((T, W), lambda b, g, i: (b * nq + jnp.minimum(i + 1, nq - 1), g)),
            pl.BlockSpec((S, W), lambda b, g, i: (b, G + g)),
            pl.BlockSpec((S, W), lambda b, g, i: (b, 2 * G + g)),
            pl.BlockSpec((T, W), lambda b, g, i: (b * nq + i, 3 * G + g)),
            pl.BlockSpec((ATT_NH, 2, T, T), lambda b, g, i: (g, 0, 0, 0)),
        ],
        out_specs=pl.BlockSpec((T, W), lambda b, g, i: (b * nq + i, g)),
        out_shape=jax.ShapeDtypeStruct((B * S, d_a), BF16),
        scratch_shapes=[pltpu.VMEM((ATT_NH, 2 * T, LANES), BF16),
                        pltpu.VMEM((ATT_NH, T, 2 * T), F32),
                        pltpu.VMEM((ATT_NH, T, 2 * T), F32),
                        pltpu.VMEM((ATT_NH, 2 * HEAD_DIM_A, 2 * T), F32),
                        pltpu.VMEM((ATT_NH, 1, 2 * T), F32),
                        pltpu.VMEM((ATT_NH, 1, 2 * T), F32)],
        compiler_params=_cparams(("parallel", "parallel", "arbitrary")),
        name="diff_attn",
    )(lamv, g_subln, proj, proj, proj, proj, proj, bias_tiles)


def _conv_kernel(u_ref, halo_ref, z_ref, cw_ref, cb_ref, lng_ref, lnb_ref, o_ref, c_scr):
    C = o_ref.shape[1]
    TS = o_ref.shape[0]

    def glu(u):
        return u[:, :C].astype(F32) * jax.nn.sigmoid(u[:, C:].astype(F32))

    halo = glu(halo_ref[...])
    c_scr[0, 0:CONV_HALO] = jnp.where(pl.program_id(1) == 0, jnp.zeros_like(halo), halo)
    c_scr[0, CONV_HALO:CONV_HALO + TS] = glu(u_ref[...])
    c_scr[0, CONV_HALO + TS:] = jnp.zeros((SUBLANES, C), F32)

    def shift(r, carry):
        r0 = pl.multiple_of(r * CONV_RS, CONV_RS)
        win = c_scr[0, pl.ds(r0, CONV_RS + SUBLANES), :]
        for s in range(1, SUBLANES):
            c_scr[s, pl.ds(r0, CONV_RS), :] = win[s:s + CONV_RS]
        return carry

    lax.fori_loop(0, (CONV_HALO + TS) // CONV_RS, shift, 0)

    base = CONV_HALO - (CONV_K - 1)

    groups = CONV_RS // SUBLANES

    def chunk(r, carry):
        r0 = pl.multiple_of(r * CONV_RS, CONV_RS)
        accs = [cb_ref[...]] * groups
        for phase in range(SUBLANES):
            tiles = {}
            for a in range((base + CONV_K - 1) // SUBLANES + 1):
                j = a * SUBLANES + phase - base
                if not 0 <= j < CONV_K:
                    continue
                wj = cw_ref[j]
                for g in range(groups):
                    if g + a not in tiles:
                        row = r0 + (g + a) * SUBLANES
                        tiles[g + a] = c_scr[phase, pl.ds(row, SUBLANES), :]
                    accs[g] = accs[g] + wj * tiles[g + a]
        z = z_ref[pl.ds(r0, CONV_RS), :].astype(F32)
        ys = []
        for g in range(groups):
            mu = jnp.mean(accs[g], axis=-1, keepdims=True)
            xc = accs[g] - mu
            var = jnp.mean(xc * xc, axis=-1, keepdims=True)
            y = xc * lax.rsqrt(var + EPS) * lng_ref[...] + lnb_ref[...]
            ys.append(_silu(y) * _silu(z[g * SUBLANES:(g + 1) * SUBLANES]))
        o_ref[pl.ds(r0, CONV_RS), :] = jnp.concatenate(ys, axis=0).astype(o_ref.dtype)
        return carry

    lax.fori_loop(0, TS // CONV_RS, chunk, 0)


def _conv_module(proj, conv_w, conv_b, ln_g, ln_b, B, S, u_off, z_off):
    C = conv_w.shape[-1]
    TS = CONV_TS
    ns = S // TS
    hb = TS // CONV_HALO
    return pl.pallas_call(
        _conv_kernel,
        grid=(B, ns),
        in_specs=[
            pl.BlockSpec((TS, 2 * C), lambda b, s: (b * ns + s, u_off // (2 * C))),
            pl.BlockSpec((CONV_HALO, 2 * C),
                         lambda b, s: (jnp.maximum((b * ns + s) * hb - 1, 0), u_off // (2 * C))),
            pl.BlockSpec((TS, C), lambda b, s: (b * ns + s, z_off // C)),
            pl.BlockSpec((CONV_K, SUBLANES, C), lambda b, s: (0, 0, 0)),
            pl.BlockSpec((SUBLANES, C), lambda b, s: (0, 0)),
            pl.BlockSpec((SUBLANES, C), lambda b, s: (0, 0)),
            pl.BlockSpec((SUBLANES, C), lambda b, s: (0, 0)),
        ],
        out_specs=pl.BlockSpec((TS, C), lambda b, s: (b * ns + s, 0)),
        out_shape=jax.ShapeDtypeStruct((B * S, C), BF16),
        scratch_shapes=[pltpu.VMEM((SUBLANES, CONV_HALO + TS + SUBLANES, C), F32)],
        compiler_params=_cparams(("parallel", "parallel")),
        name="conv_module",
    )(proj, proj, proj, conv_w, conv_b, ln_g, ln_b)


def _memkv_kernel(mem_ref, g_ref, w_ref, o_ref):
    xf = mem_ref[...]
    ms = jnp.mean(xf * xf, axis=-1, keepdims=True)
    mn = (xf * lax.rsqrt(ms + EPS) * g_ref[...]).astype(BF16)
    o_ref[...] = jnp.dot(mn, w_ref[...], preferred_element_type=F32).astype(o_ref.dtype)


def _mem_kv(mem2, g_mem, w_mem_kv, n_mem):
    TM, D = mem2.shape
    N = w_mem_kv.shape[1]
    return pl.pallas_call(
        _memkv_kernel,
        grid=(TM // n_mem,),
        in_specs=[
            pl.BlockSpec((n_mem, D), lambda b: (b, 0)),
            pl.BlockSpec((1, D), lambda b: (0, 0)),
            pl.BlockSpec((D, N), lambda b: (0, 0)),
        ],
        out_specs=pl.BlockSpec((n_mem, N), lambda b: (b, 0)),
        out_shape=jax.ShapeDtypeStruct((TM, N), BF16),
        compiler_params=_cparams(("parallel",)),
        name="mem_kv",
    )(mem2, g_mem, w_mem_kv)


def _xattn_kernel(q_ref, z_ref, kv_ref, o_ref):
    dx = HEAD_DIM_X
    d_x = N_HEADS_X * dx
    scale = jnp.asarray(dx ** -0.5, BF16)
    for h in range(N_HEADS_X):
        qh = q_ref[:, h * dx:(h + 1) * dx] * scale
        kh = kv_ref[:, h * dx:(h + 1) * dx]
        vh = kv_ref[:, d_x + h * dx:d_x + (h + 1) * dx]
        s = lax.dot_general(qh, kh, (((1,), (1,)), ((), ())), preferred_element_type=F32)
        m = jnp.max(s, axis=-1, keepdims=True)
        p = jnp.exp(s - m)
        l = jnp.sum(p, axis=-1, keepdims=True)
        o = jnp.dot(p.astype(BF16), vh, preferred_element_type=F32) / l
        z = z_ref[:, h * dx:(h + 1) * dx].astype(F32)
        o_ref[:, h * dx:(h + 1) * dx] = (o * _silu(z)).astype(o_ref.dtype)


def _cross_attention(proj, kv, B, S, n_mem, q_off, z_off):
    d_x = N_HEADS_X * HEAD_DIM_X
    TQ = XATT_TQ
    nq = S // TQ
    return pl.pallas_call(
        _xattn_kernel,
        grid=(B, nq),
        in_specs=[
            pl.BlockSpec((TQ, d_x), lambda b, i: (b * nq + i, q_off // d_x)),
            pl.BlockSpec((TQ, d_x), lambda b, i: (b * nq + i, z_off // d_x)),
            pl.BlockSpec((n_mem, 2 * d_x), lambda b, i: (b, 0)),
        ],
        out_specs=pl.BlockSpec((TQ, d_x), lambda b, i: (b * nq + i, 0)),
        out_shape=jax.ShapeDtypeStruct((B * S, d_x), BF16),
        compiler_params=_cparams(("parallel", "parallel")),
        name="mem_xattn",
    )(proj, proj, kv)


def _merge_kernel(a_ref, b_ref, c_ref, gt_ref, x_ref, woa_ref, wob_ref, woc_ref, wout_ref,
                  bm_ref, gp_ref, o_ref):
    D = x_ref.shape[1]
    merged = jnp.zeros(x_ref.shape, F32)
    for n, (br_ref, w_ref) in enumerate(((a_ref, woa_ref), (b_ref, wob_ref), (c_ref, woc_ref))):
        yb = jnp.dot(br_ref[...], w_ref[...], preferred_element_type=F32)
        gate = jax.nn.sigmoid(gt_ref[:, n * D:(n + 1) * D].astype(F32) + bm_ref[:, n * D:(n + 1) * D])
        merged = merged + gate * yb
    y = jnp.dot(merged.astype(BF16), wout_ref[...], preferred_element_type=F32)
    ms = jnp.mean(y * y, axis=-1, keepdims=True)
    o_ref[...] = x_ref[...] + y * lax.rsqrt(ms + EPS) * gp_ref[...]


def _merge(ya, yb, yc, proj, x2, w_oa, w_ob, w_oc, w_out, b_merge, g_post, g_off):
    T, D = x2.shape
    TM = MERGE_TM
    row = lambda i: (i, 0)
    fixed = lambda i: (0, 0)
    return pl.pallas_call(
        _merge_kernel,
        grid=(T // TM,),
        in_specs=[
            pl.BlockSpec((TM, D), row),
            pl.BlockSpec((TM, D), row),
            pl.BlockSpec((TM, D), row),
            pl.BlockSpec((TM, N_BRANCH * D), lambda i: (i, g_off // (N_BRANCH * D))),
            pl.BlockSpec((TM, D), row),
            pl.BlockSpec((D, D), fixed),
            pl.BlockSpec((D, D), fixed),
            pl.BlockSpec((D, D), fixed),
            pl.BlockSpec((D, D), fixed),
            pl.BlockSpec((1, N_BRANCH * D), fixed),
            pl.BlockSpec((1, D), fixed),
        ],
        out_specs=pl.BlockSpec((TM, D), row),
        out_shape=jax.ShapeDtypeStruct((T, D), F32),
        compiler_params=_cparams(("parallel",)),
        name="merge_out",
    )(ya, yb, yc, proj, x2, w_oa, w_ob, w_oc, w_out, b_merge, g_post)


def kernel(x, mem, rel_bias, g_pre, g_mem, w_in, b_merge, lam_q1, lam_k1, lam_q2, lam_k2,
           g_subln, w_oa, conv_w, conv_b, ln_g, ln_b, w_ob, w_mem_kv, w_oc, w_out, g_post):
    B, S, D = x.shape
    n_mem = mem.shape[1]
    depth = w_in.shape[0]
    d_a = N_HEADS_A * 2 * HEAD_DIM_A
    d_conv = conv_w.shape[2]
    d_x = N_HEADS_X * HEAD_DIM_X
    sizes = (d_a, d_a, d_a, d_a, 2 * d_conv, d_conv, d_x, d_x, N_BRANCH * D)
    offs = [int(o) for o in np.cumsum((0,) + sizes)]
    assert offs[-1] == w_in.shape[2] and d_a == d_conv == d_x == D
    assert MAX_DIST <= ATT_T and ATT_T % CHUNK == 0 and CONV_HALO >= CONV_K - 1

    x2 = x.reshape(B * S, D)
    mem2 = mem.reshape(B * n_mem, D)
    bias_tiles = _bias_tiles(rel_bias)
    col_scale = jnp.ones((SUBLANES, offs[-1]), F32).at[:, :d_a].set(Q_SCALE_LOG2)
    for l in range(depth):
        lam_init = 0.8 - 0.6 * math.exp(-0.3 * l)
        proj = _proj(x2, g_pre[l][None], w_in[l].astype(BF16), col_scale)

        lamv = jnp.stack([lam_q1[l], lam_k1[l], lam_q2[l], lam_k2[l]])
        ya = _diff_attention(proj, bias_tiles, lamv, g_subln[l][None], B, S, lam_init)

        rep = lambda v: jnp.broadcast_to(v[..., None, :], v.shape[:-1] + (SUBLANES, v.shape[-1]))
        yb = _conv_module(proj, rep(conv_w[l]), rep(conv_b[l]), rep(ln_g[l]), rep(ln_b[l]),
                          B, S, offs[4], offs[5])

        kv = _mem_kv(mem2, g_mem[l][None], w_mem_kv[l].astype(BF16), n_mem)
        yc = _cross_attention(proj, kv, B, S, n_mem, offs[6], offs[7])

        x2 = _merge(ya, yb, yc, proj, x2, w_oa[l].astype(BF16), w_ob[l].astype(BF16),
                    w_oc[l].astype(BF16), w_out[l].astype(BF16), b_merge[l][None], g_post[l][None],
                    offs[8])
    return x2.reshape(B, S, D)
```

```python
import functools
import math

import jax
import jax.numpy as jnp
import numpy as np
from jax import lax
from jax.experimental import pallas as pl
from jax.experimental.pallas import tpu as pltpu

CHUNK = 64
N_HEADS_A = 8
HEAD_DIM_A = 64
N_HEADS_X = 4
HEAD_DIM_X = 256
CONV_K = 31
N_BRANCH = 3
N_BUCKETS = 32
MAX_DIST = 128
EPS = 1e-6
NEG_INF = -1e30
LOG2_E = math.log2(math.e)
Q_SCALE_LOG2 = HEAD_DIM_A ** -0.5 * LOG2_E

LANES = 128
SUBLANES = 8
VMEM_LIMIT_BYTES = 48 * 1024 * 1024

PROJ_TM = 1024
PROJ_TN = 4096
ATT_T = 256
ATT_NH = 8
ATT_SKEW = 1
CONV_TS = 512
CONV_HALO = 32
CONV_RS = 32
CONV_LN_ROWS = 128
CONV_ROWS = 64
XATT_TQ = 512
MERGE_TM = 512

BF16 = jnp.bfloat16
F32 = jnp.float32


def _cparams(sem):
    return pltpu.CompilerParams(dimension_semantics=sem, vmem_limit_bytes=VMEM_LIMIT_BYTES)


def _silu(z):
    return z * jax.nn.sigmoid(z)


def _proj_kernel(x_ref, g_ref, w_ref, cs_ref, o_ref, h_scr):
    @pl.when(pl.program_id(1) == 0)
    def _():
        xf = x_ref[...]
        ms = jnp.mean(xf * xf, axis=-1, keepdims=True)
        h_scr[...] = (xf * lax.rsqrt(ms + EPS) * g_ref[...]).astype(BF16)

    acc = jnp.dot(h_scr[...], w_ref[...], preferred_element_type=F32)
    tm, tn = acc.shape
    acc = acc.reshape(tm // SUBLANES, SUBLANES, tn) * cs_ref[...][None]
    o_ref[...] = acc.reshape(tm, tn).astype(o_ref.dtype)


def _proj(x2, g_pre, w_in, col_scale):
    T, D = x2.shape
    N = w_in.shape[1]
    return pl.pallas_call(
        _proj_kernel,
        grid=(T // PROJ_TM, N // PROJ_TN),
        in_specs=[
            pl.BlockSpec((PROJ_TM, D), lambda i, j: (i, 0)),
            pl.BlockSpec((1, D), lambda i, j: (0, 0)),
            pl.BlockSpec((D, PROJ_TN), lambda i, j: (0, j)),
            pl.BlockSpec((SUBLANES, PROJ_TN), lambda i, j: (0, j)),
        ],
        out_specs=pl.BlockSpec((PROJ_TM, PROJ_TN), lambda i, j: (i, j)),
        out_shape=jax.ShapeDtypeStruct((T, N), BF16),
        scratch_shapes=[pltpu.VMEM((PROJ_TM, D), BF16)],
        compiler_params=_cparams(("parallel", "arbitrary")),
        name="proj",
    )(x2, g_pre, w_in, col_scale)


def _t5_bucket(rel):
    nb = N_BUCKETS // 2
    max_exact = nb // 2
    ret = (rel > 0).astype(jnp.int32) * nb
    n = jnp.abs(rel)
    nf = jnp.maximum(n, 1).astype(jnp.float32)
    large = max_exact + (jnp.log(nf / max_exact) / math.log(MAX_DIST / max_exact)
                         * (nb - max_exact)).astype(jnp.int32)
    large = jnp.minimum(large, nb - 1)
    return ret + jnp.where(n < max_exact, n, large)


def _bucket_tiles():
    kpos = jnp.arange(ATT_T, dtype=jnp.int32)[:, None]
    qpos = jnp.arange(ATT_T, dtype=jnp.int32)[None, :]
    tiles = []
    for d in (0, -1, -2):
        rel = kpos + d * ATT_T - qpos
        tiles.append(_t5_bucket(rel))
    allowed = (kpos // CHUNK) <= (qpos // CHUNK)
    tiles[0] = jnp.where(allowed, tiles[0], -1)
    return jnp.stack(tiles)


def _bias_kernel(relb_ref, bkt_ref, o_ref):
    h = pl.program_id(0)
    bkt = bkt_ref[...]
    res = jnp.zeros(bkt.shape, F32)
    for b in range(N_BUCKETS):
        res = jnp.where(bkt == b, relb_ref[b, h], res)
    near = res[0:2] - res[2, 0:1, 0:1]
    o_ref[0] = jnp.where(bkt[0:2] < 0, NEG_INF, near * LOG2_E)


def _bias_tiles(rel_bias):
    bkt = _bucket_tiles()
    return pl.pallas_call(
        _bias_kernel,
        grid=(N_HEADS_A,),
        in_specs=[
            pl.BlockSpec(memory_space=pltpu.SMEM),
            pl.BlockSpec((3, ATT_T, ATT_T), lambda h: (0, 0, 0)),
        ],
        out_specs=pl.BlockSpec((1, 2, ATT_T, ATT_T), lambda h: (h, 0, 0, 0)),
        out_shape=jax.ShapeDtypeStruct((N_HEADS_A, 2, ATT_T, ATT_T), F32),
        compiler_params=_cparams(("arbitrary",)),
        name="bias_tiles",
    )(rel_bias, bkt)


def _attn_kernel(lamv_ref, gs_ref, q_ref, qn_ref, k_ref, v_ref, z_ref, bias_ref, o_ref,
                 qq_scr, sta_scr, stb_scr, acc_scr, m_scr, l_scr, *, lam_init):
    T = ATT_T
    qi = pl.program_id(2)
    ones = jnp.ones((SUBLANES, T), BF16)

    all_heads = tuple(range(ATT_NH))

    def scores(t, st_ref, heads=all_heads):
        k0 = pl.multiple_of(t * T, T)
        for hh in heads:
            kb = k_ref[pl.ds(k0, T), hh * LANES:(hh + 1) * LANES]
            st_ref[hh] = lax.dot_general(kb, qq_scr[hh], (((1,), (1,)), ((), ())),
                                         preferred_element_type=F32)

    def start_block(src_ref):
        lane = lax.broadcasted_iota(jnp.int32, (T, LANES), 1)
        for hh in range(ATT_NH):
            q = src_ref[:, hh * LANES:(hh + 1) * LANES]
            zero = jnp.zeros_like(q)
            qq_scr[hh, 0:T] = jnp.where(lane < HEAD_DIM_A, q, zero)
            qq_scr[hh, T:2 * T] = jnp.where(lane >= HEAD_DIM_A, q, zero)
        m_scr[...] = jnp.full(m_scr.shape, -jnp.inf, F32)
        l_scr[...] = jnp.zeros(l_scr.shape, F32)
        acc_scr[...] = jnp.zeros(acc_scr.shape, F32)
        scores(0, sta_scr)

    @pl.when(qi == 0)
    def _():
        start_block(q_ref)

    def softmax_pv(t, st_ref, tile, heads=all_heads):
        k0 = pl.multiple_of(t * T, T)
        for hh in heads:
            st = st_ref[hh]
            if tile is not None:
                bias = bias_ref[hh, tile]
                st = st + jnp.concatenate([bias, bias], axis=1)
            m_prev = m_scr[hh]
            m_new = jnp.maximum(m_prev, jnp.max(st, axis=0, keepdims=True))
            alpha = jnp.exp2(m_prev - m_new)
            p = jnp.exp2(st - m_new).astype(BF16)
            vb = v_ref[pl.ds(k0, T), hh * LANES:(hh + 1) * LANES]
            l_scr[hh] = alpha * l_scr[hh] + jnp.dot(ones, p, preferred_element_type=F32)[0:1]
            pv = lax.dot_general(vb, p, (((0,), (0,)), ((), ())), preferred_element_type=F32)
            acc_scr[hh] = alpha * acc_scr[hh] + pv
            m_scr[hh] = m_new

    def tick(t, cur, nxt, tile):
        scores(t + 1, nxt, all_heads[:ATT_SKEW])
        for hh in all_heads:
            scores(t + 1, nxt, all_heads[hh + ATT_SKEW:hh + ATT_SKEW + 1])
            softmax_pv(t, cur, tile, (hh,))

    n_far = jnp.maximum(qi - 1, 0)

    def far_pair(i, carry):
        tick(2 * i, sta_scr, stb_scr, None)
        tick(2 * i + 1, stb_scr, sta_scr, None)
        return carry

    lax.fori_loop(0, n_far // 2, far_pair, 0)
    odd = lax.rem(n_far, 2) == 1

    @pl.when(odd)
    def _():
        tick(n_far - 1, sta_scr, stb_scr, None)

    @pl.when(jnp.logical_and(qi >= 1, jnp.logical_not(odd)))
    def _():
        tick(qi - 1, sta_scr, stb_scr, 1)
        softmax_pv(qi, stb_scr, 0)

    @pl.when(jnp.logical_and(qi >= 1, odd))
    def _():
        tick(qi - 1, stb_scr, sta_scr, 1)
        softmax_pv(qi, sta_scr, 0)

    @pl.when(qi == 0)
    def _():
        softmax_pv(0, sta_scr, 0)

    lv = lamv_ref[...]
    lam = (jnp.exp(jnp.sum(lv[0:1] * lv[1:2], axis=-1, keepdims=True))
           - jnp.exp(jnp.sum(lv[2:3] * lv[3:4], axis=-1, keepdims=True)) + lam_init)
    for hh in range(ATT_NH):
        ot = acc_scr[hh] / l_scr[hh]
        ot = ot[:, :T] - lam * ot[:, T:]
        ms = jnp.mean(ot * ot, axis=0, keepdims=True)
        on = (ot * lax.rsqrt(ms + EPS)).T * gs_ref[...] * (1.0 - lam_init)
        z = z_ref[:, hh * LANES:(hh + 1) * LANES].astype(F32)
        o_ref[:, hh * LANES:(hh + 1) * LANES] = (on * _silu(z)).astype(o_ref.dtype)

    start_block(qn_ref)


def _diff_attention(proj, bias_tiles, lamv, g_subln, B, S, lam_init):
    T = ATT_T
    nq = S // T
    H = N_HEADS_A
    d_a = H * 2 * HEAD_DIM_A
    W = ATT_NH * LANES
    G = H // ATT_NH
    return pl.pallas_call(
        functools.partial(_attn_kernel, lam_init=lam_init),
        grid=(B, G, nq),
        in_specs=[
            pl.BlockSpec((4, HEAD_DIM_A), lambda b, g, i: (0, 0)),
            pl.BlockSpec((1, LANES), lambda b, g, i: (0, 0)),
            pl.BlockSpec((T, W), lambda b, g, i: (b * nq + i, g)),
            pl.BlockSpec((T, W), lambda b, g, i: (b * nq + jnp.minimum(i + 1, nq - 1), g)),
            pl.BlockSpec((S, W), lambda b, g, i: (b, G + g)),
            pl.BlockSpec((S, W), lambda b, g, i: (b, 2 * G + g)),
            pl.BlockSpec((T, W), lambda b, g, i: (b * nq + i, 3 * G + g)),
            pl.BlockSpec((ATT_NH, 2, T, T), lambda b, g, i: (g, 0, 0, 0)),
        ],
        out_specs=pl.BlockSpec((T, W), lambda b, g, i: (b * nq + i, g)),
        out_shape=jax.ShapeDtypeStruct((B * S, d_a), BF16),
        scratch_shapes=[pltpu.VMEM((ATT_NH, 2 * T, LANES), BF16),
                        pltpu.VMEM((ATT_NH, T, 2 * T), F32),
                        pltpu.VMEM((ATT_NH, T, 2 * T), F32),
                        pltpu.VMEM((ATT_NH, 2 * HEAD_DIM_A, 2 * T), F32),
                        pltpu.VMEM((ATT_NH, 1, 2 * T), F32),
                        pltpu.VMEM((ATT_NH, 1, 2 * T), F32)],
        compiler_params=_cparams(("parallel", "parallel", "arbitrary")),
        name="diff_attn",
    )(lamv, g_subln, proj, proj, proj, proj, proj, bias_tiles)


def _conv_kernel(u_ref, halo_ref, z_ref, cw_ref, cb_ref, lng_ref, lnb_ref, o_ref, c_scr, y_scr):
    C = o_ref.shape[1]
    TS = o_ref.shape[0]

    def glu(u):
        return u[:, :C].astype(F32) * jax.nn.sigmoid(u[:, C:].astype(F32))

    halo = glu(halo_ref[...])
    c_scr[0, 0:CONV_HALO] = jnp.where(pl.program_id(1) == 0, jnp.zeros_like(halo), halo)
    c_scr[0, CONV_HALO:CONV_HALO + TS] = glu(u_ref[...])
    c_scr[0, CONV_HALO + TS:] = jnp.zeros((SUBLANES, C), F32)

    def shift(r, carry):
        r0 = pl.multiple_of(r * CONV_RS, CONV_RS)
        win = c_scr[0, pl.ds(r0, CONV_RS + SUBLANES), :]
        for s in range(1, SUBLANES):
            c_scr[s, pl.ds(r0, CONV_RS), :] = win[s:s + CONV_RS]
        return carry

    lax.fori_loop(0, (CONV_HALO + TS) // CONV_RS, shift, 0)

    base = CONV_HALO - (CONV_K - 1)

    for lb in range(C // LANES):
        lanes = slice(lb * LANES, (lb + 1) * LANES)
        taps = [cw_ref[j, :, lanes] for j in range(CONV_K)]
        bias = cb_ref[:, lanes]

        def conv_rows(r, carry, lanes=lanes, taps=taps, bias=bias):
            r0 = pl.multiple_of(r * CONV_ROWS, CONV_ROWS)
            for g in range(CONV_ROWS // SUBLANES):
                acc = bias
                for j in range(CONV_K):
                    off = base + j
                    row = r0 + g * SUBLANES + off - off % SUBLANES
                    acc = acc + taps[j] * c_scr[off % SUBLANES, pl.ds(row, SUBLANES), lanes]
                y_scr[pl.ds(r0 + g * SUBLANES, SUBLANES), lanes] = acc
            return carry

        lax.fori_loop(0, TS // CONV_ROWS, conv_rows, 0)

    pack = 2 * SUBLANES

    def norm_rows(r, carry):
        r0 = pl.multiple_of(r * CONV_LN_ROWS, CONV_LN_ROWS)
        for s in range(CONV_LN_ROWS // pack):
            rs = r0 + s * pack
            z = z_ref[pl.ds(rs, pack), :].astype(F32)
            ys = []
            for g in range(pack // SUBLANES):
                acc = y_scr[pl.ds(rs + g * SUBLANES, SUBLANES), :]
                mu = jnp.mean(acc, axis=-1, keepdims=True)
                xc = acc - mu
                var = jnp.mean(xc * xc, axis=-1, keepdims=True)
                y = xc * lax.rsqrt(var + EPS) * lng_ref[...] + lnb_ref[...]
                ys.append(_silu(y) * _silu(z[g * SUBLANES:(g + 1) * SUBLANES]))
            o_ref[pl.ds(rs, pack), :] = jnp.concatenate(ys, axis=0).astype(o_ref.dtype)
        return carry

    lax.fori_loop(0, TS // CONV_LN_ROWS, norm_rows, 0)


def _conv_module(proj, conv_w, conv_b, ln_g, ln_b, B, S, u_off, z_off):
    C = conv_w.shape[-1]
    TS = CONV_TS
    ns = S // TS
    hb = TS // CONV_HALO
    return pl.pallas_call(
        _conv_kernel,
        grid=(B, ns),
        in_specs=[
            pl.BlockSpec((TS, 2 * C), lambda b, s: (b * ns + s, u_off // (2 * C))),
            pl.BlockSpec((CONV_HALO, 2 * C),
                         lambda b, s: (jnp.maximum((b * ns + s) * hb - 1, 0), u_off // (2 * C))),
            pl.BlockSpec((TS, C), lambda b, s: (b * ns + s, z_off // C)),
            pl.BlockSpec((CONV_K, SUBLANES, C), lambda b, s: (0, 0, 0)),
            pl.BlockSpec((SUBLANES, C), lambda b, s: (0, 0)),
            pl.BlockSpec((SUBLANES, C), lambda b, s: (0, 0)),
            pl.BlockSpec((SUBLANES, C), lambda b, s: (0, 0)),
        ],
        out_specs=pl.BlockSpec((TS, C), lambda b, s: (b * ns + s, 0)),
        out_shape=jax.ShapeDtypeStruct((B * S, C), BF16),
        scratch_shapes=[pltpu.VMEM((SUBLANES, CONV_HALO + TS + SUBLANES, C), F32),
                        pltpu.VMEM((TS, C), F32)],
        compiler_params=_cparams(("parallel", "parallel")),
        name="conv_module",
    )(proj, proj, proj, conv_w, conv_b, ln_g, ln_b)


def _memkv_kernel(mem_ref, g_ref, w_ref, o_ref):
    xf = mem_ref[...]
    ms = jnp.mean(xf * xf, axis=-1, keepdims=True)
    mn = (xf * lax.rsqrt(ms + EPS) * g_ref[...]).astype(BF16)
    o_ref[...] = jnp.dot(mn, w_ref[...], preferred_element_type=F32).astype(o_ref.dtype)


def _mem_kv(mem2, g_mem, w_mem_kv, n_mem):
    TM, D = mem2.shape
    N = w_mem_kv.shape[1]
    return pl.pallas_call(
        _memkv_kernel,
        grid=(TM // n_mem,),
        in_specs=[
            pl.BlockSpec((n_mem, D), lambda b: (b, 0)),
            pl.BlockSpec((1, D), lambda b: (0, 0)),
            pl.BlockSpec((D, N), lambda b: (0, 0)),
        ],
        out_specs=pl.BlockSpec((n_mem, N), lambda b: (b, 0)),
        out_shape=jax.ShapeDtypeStruct((TM, N), BF16),
        compiler_params=_cparams(("parallel",)),
        name="mem_kv",
    )(mem2, g_mem, w_mem_kv)


def _xattn_kernel(q_ref, z_ref, kv_ref, o_ref):
    dx = HEAD_DIM_X
    d_x = N_HEADS_X * dx
    scale = jnp.asarray(dx ** -0.5, BF16)
    for h in range(N_HEADS_X):
        qh = q_ref[:, h * dx:(h + 1) * dx] * scale
        kh = kv_ref[:, h * dx:(h + 1) * dx]
        vh = kv_ref[:, d_x + h * dx:d_x + (h + 1) * dx]
        s = lax.dot_general(qh, kh, (((1,), (1,)), ((), ())), preferred_element_type=F32)
        m = jnp.max(s, axis=-1, keepdims=True)
        p = jnp.exp(s - m)
        l = jnp.sum(p, axis=-1, keepdims=True)
        o = jnp.dot(p.astype(BF16), vh, preferred_element_type=F32) / l
        z = z_ref[:, h * dx:(h + 1) * dx].astype(F32)
        o_ref[:, h * dx:(h + 1) * dx] = (o * _silu(z)).astype(o_ref.dtype)


def _cross_attention(proj, kv, B, S, n_mem, q_off, z_off):
    d_x = N_HEADS_X * HEAD_DIM_X
    TQ = XATT_TQ
    nq = S // TQ
    return pl.pallas_call(
        _xattn_kernel,
        grid=(B, nq),
        in_specs=[
            pl.BlockSpec((TQ, d_x), lambda b, i: (b * nq + i, q_off // d_x)),
            pl.BlockSpec((TQ, d_x), lambda b, i: (b * nq + i, z_off // d_x)),
            pl.BlockSpec((n_mem, 2 * d_x), lambda b, i: (b, 0)),
        ],
        out_specs=pl.BlockSpec((TQ, d_x), lambda b, i: (b * nq + i, 0)),
        out_shape=jax.ShapeDtypeStruct((B * S, d_x), BF16),
        compiler_params=_cparams(("parallel", "parallel")),
        name="mem_xattn",
    )(proj, proj, kv)


def _merge_kernel(a_ref, b_ref, c_ref, gt_ref, x_ref, woa_ref, wob_ref, woc_ref, wout_ref,
                  bm_ref, gp_ref, o_ref):
    D = x_ref.shape[1]
    merged = jnp.zeros(x_ref.shape, F32)
    for n, (br_ref, w_ref) in enumerate(((a_ref, woa_ref), (b_ref, wob_ref), (c_ref, woc_ref))):
        yb = jnp.dot(br_ref[...], w_ref[...], preferred_element_type=F32)
        gate = jax.nn.sigmoid(gt_ref[:, n * D:(n + 1) * D].astype(F32) + bm_ref[:, n * D:(n + 1) * D])
        merged = merged + gate * yb
    y = jnp.dot(merged.astype(BF16), wout_ref[...], preferred_element_type=F32)
    ms = jnp.mean(y * y, axis=-1, keepdims=True)
    o_ref[...] = x_ref[...] + y * lax.rsqrt(ms + EPS) * gp_ref[...]


def _merge(ya, yb, yc, proj, x2, w_oa, w_ob, w_oc, w_out, b_merge, g_post, g_off):
    T, D = x2.shape
    TM = MERGE_TM
    row = lambda i: (i, 0)
    fixed = lambda i: (0, 0)
    return pl.pallas_call(
        _merge_kernel,
        grid=(T // TM,),
        in_specs=[
            pl.BlockSpec((TM, D), row),
            pl.BlockSpec((TM, D), row),
            pl.BlockSpec((TM, D), row),
            pl.BlockSpec((TM, N_BRANCH * D), lambda i: (i, g_off // (N_BRANCH * D))),
            pl.BlockSpec((TM, D), row),
            pl.BlockSpec((D, D), fixed),
            pl.BlockSpec((D, D), fixed),
            pl.BlockSpec((D, D), fixed),
            pl.BlockSpec((D, D), fixed),
            pl.BlockSpec((1, N_BRANCH * D), fixed),
            pl.BlockSpec((1, D), fixed),
        ],
        out_specs=pl.BlockSpec((TM, D), row),
        out_shape=jax.ShapeDtypeStruct((T, D), F32),
        compiler_params=_cparams(("parallel",)),
        name="merge_out",
    )(ya, yb, yc, proj, x2, w_oa, w_ob, w_oc, w_out, b_merge, g_post)


def kernel(x, mem, rel_bias, g_pre, g_mem, w_in, b_merge, lam_q1, lam_k1, lam_q2, lam_k2,
           g_subln, w_oa, conv_w, conv_b, ln_g, ln_b, w_ob, w_mem_kv, w_oc, w_out, g_post):
    B, S, D = x.shape
    n_mem = mem.shape[1]
    depth = w_in.shape[0]
    d_a = N_HEADS_A * 2 * HEAD_DIM_A
    d_conv = conv_w.shape[2]
    d_x = N_HEADS_X * HEAD_DIM_X
    sizes = (d_a, d_a, d_a, d_a, 2 * d_conv, d_conv, d_x, d_x, N_BRANCH * D)
    offs = [int(o) for o in np.cumsum((0,) + sizes)]
    assert offs[-1] == w_in.shape[2] and d_a == d_conv == d_x == D
    assert MAX_DIST <= ATT_T and ATT_T % CHUNK == 0 and CONV_HALO >= CONV_K - 1

    x2 = x.reshape(B * S, D)
    mem2 = mem.reshape(B * n_mem, D)
    bias_tiles = _bias_tiles(rel_bias)
    col_scale = jnp.ones((SUBLANES, offs[-1]), F32).at[:, :d_a].set(Q_SCALE_LOG2)
    for l in range(depth):
        lam_init = 0.8 - 0.6 * math.exp(-0.3 * l)
        proj = _proj(x2, g_pre[l][None], w_in[l].astype(BF16), col_scale)

        lamv = jnp.stack([lam_q1[l], lam_k1[l], lam_q2[l], lam_k2[l]])
        ya = _diff_attention(proj, bias_tiles, lamv, g_subln[l][None], B, S, lam_init)

        rep = lambda v: jnp.broadcast_to(v[..., None, :], v.shape[:-1] + (SUBLANES, v.shape[-1]))
        yb = _conv_module(proj, rep(conv_w[l]), rep(conv_b[l]), rep(ln_g[l]), rep(ln_b[l]),
                          B, S, offs[4], offs[5])

        kv = _mem_kv(mem2, g_mem[l][None], w_mem_kv[l].astype(BF16), n_mem)
        yc = _cross_attention(proj, kv, B, S, n_mem, offs[6], offs[7])

        x2 = _merge(ya, yb, yc, proj, x2, w_oa[l].astype(BF16), w_ob[l].astype(BF16),
                    w_oc[l].astype(BF16), w_out[l].astype(BF16), b_merge[l][None], g_post[l][None],
                    offs[8])
    return x2.reshape(B, S, D)
```

```python
import functools
import math

import jax
import jax.numpy as jnp
import numpy as np
from jax import lax
from jax.experimental import pallas as pl
from jax.experimental.pallas import tpu as pltpu

CHUNK = 64
N_HEADS_A = 8
HEAD_DIM_A = 64
N_HEADS_X = 4
HEAD_DIM_X = 256
CONV_K = 31
N_BRANCH = 3
N_BUCKETS = 32
MAX_DIST = 128
EPS = 1e-6
NEG_INF = -1e30
LOG2_E = math.log2(math.e)
Q_SCALE_LOG2 = HEAD_DIM_A ** -0.5 * LOG2_E
XQ_SCALE_LOG2 = HEAD_DIM_X ** -0.5 * LOG2_E

LANES = 128
SUBLANES = 8
VMEM_LIMIT_BYTES = 48 * 1024 * 1024

PROJ_TM = 1024
PROJ_TN = 4096
ATT_T = 256
ATT_NH = 8
ATT_SKEW = 1
CONV_TS = 512
CONV_HALO = 32
CONV_RS = 32
CONV_LN_ROWS = 128
CONV_ROWS = 64
XATT_TQ = 512
MERGE_TM = 512

BF16 = jnp.bfloat16
F32 = jnp.float32


def _cparams(sem):
    return pltpu.CompilerParams(dimension_semantics=sem, vmem_limit_bytes=VMEM_LIMIT_BYTES)


def _sigmoid(z):
    return 0.5 * jnp.tanh(0.5 * z) + 0.5


def _silu(z):
    return z * _sigmoid(z)


def _proj_kernel(x_ref, g_ref, w_ref, cs_ref, o_ref, h_scr):
    @pl.when(pl.program_id(1) == 0)
    def _():
        xf = x_ref[...]
        ms = jnp.mean(xf * xf, axis=-1, keepdims=True)
        h_scr[...] = (xf * lax.rsqrt(ms + EPS) * g_ref[...]).astype(BF16)

    acc = jnp.dot(h_scr[...], w_ref[...], preferred_element_type=F32)
    tm, tn = acc.shape
    acc = acc.reshape(tm // SUBLANES, SUBLANES, tn) * cs_ref[...][None]
    o_ref[...] = acc.reshape(tm, tn).astype(o_ref.dtype)


def _proj(x2, g_pre, w_in, col_scale):
    T, D = x2.shape
    N = w_in.shape[1]
    return pl.pallas_call(
        _proj_kernel,
        grid=(T // PROJ_TM, N // PROJ_TN),
        in_specs=[
            pl.BlockSpec((PROJ_TM, D), lambda i, j: (i, 0)),
            pl.BlockSpec((1, D), lambda i, j: (0, 0)),
            pl.BlockSpec((D, PROJ_TN), lambda i, j: (0, j)),
            pl.BlockSpec((SUBLANES, PROJ_TN), lambda i, j: (0, j)),
        ],
        out_specs=pl.BlockSpec((PROJ_TM, PROJ_TN), lambda i, j: (i, j)),
        out_shape=jax.ShapeDtypeStruct((T, N), BF16),
        scratch_shapes=[pltpu.VMEM((PROJ_TM, D), BF16)],
        compiler_params=_cparams(("parallel", "arbitrary")),
        name="proj",
    )(x2, g_pre, w_in, col_scale)


def _t5_bucket(rel):
    nb = N_BUCKETS // 2
    max_exact = nb // 2
    ret = (rel > 0).astype(jnp.int32) * nb
    n = jnp.abs(rel)
    nf = jnp.maximum(n, 1).astype(jnp.float32)
    large = max_exact + (jnp.log(nf / max_exact) / math.log(MAX_DIST / max_exact)
                         * (nb - max_exact)).astype(jnp.int32)
    large = jnp.minimum(large, nb - 1)
    return ret + jnp.where(n < max_exact, n, large)


def _bucket_tiles():
    kpos = jnp.arange(ATT_T, dtype=jnp.int32)[:, None]
    qpos = jnp.arange(ATT_T, dtype=jnp.int32)[None, :]
    tiles = []
    for d in (0, -1, -2):
        rel = kpos + d * ATT_T - qpos
        tiles.append(_t5_bucket(rel))
    allowed = (kpos // CHUNK) <= (qpos // CHUNK)
    tiles[0] = jnp.where(allowed, tiles[0], -1)
    return jnp.stack(tiles)


def _bias_kernel(relb_ref, bkt_ref, o_ref):
    h = pl.program_id(0)
    bkt = bkt_ref[...]
    res = jnp.zeros(bkt.shape, F32)
    for b in range(N_BUCKETS):
        res = jnp.where(bkt == b, relb_ref[b, h], res)
    near = res[0:2] - res[2, 0:1, 0:1]
    o_ref[0] = jnp.where(bkt[0:2] < 0, NEG_INF, near * LOG2_E)


def _bias_tiles(rel_bias):
    bkt = _bucket_tiles()
    return pl.pallas_call(
        _bias_kernel,
        grid=(N_HEADS_A,),
        in_specs=[
            pl.BlockSpec(memory_space=pltpu.SMEM),
            pl.BlockSpec((3, ATT_T, ATT_T), lambda h: (0, 0, 0)),
        ],
        out_specs=pl.BlockSpec((1, 2, ATT_T, ATT_T), lambda h: (h, 0, 0, 0)),
        out_shape=jax.ShapeDtypeStruct((N_HEADS_A, 2, ATT_T, ATT_T), F32),
        compiler_params=_cparams(("arbitrary",)),
        name="bias_tiles",
    )(rel_bias, bkt)


def _attn_kernel(lamv_ref, gs_ref, q_ref, qn_ref, k_ref, v_ref, z_ref, bias_ref, o_ref,
                 qq_scr, sta_scr, stb_scr, acc_scr, m_scr, l_scr, *, lam_init):
    T = ATT_T
    qi = pl.program_id(2)
    ones = jnp.ones((SUBLANES, T), BF16)

    all_heads = tuple(range(ATT_NH))

    def scores(t, st_ref, heads=all_heads):
        k0 = pl.multiple_of(t * T, T)
        for hh in heads:
            kb = k_ref[pl.ds(k0, T), hh * LANES:(hh + 1) * LANES]
            st_ref[hh] = lax.dot_general(kb, qq_scr[hh], (((1,), (1,)), ((), ())),
                                         preferred_element_type=F32)

    def start_block(src_ref):
        lane = lax.broadcasted_iota(jnp.int32, (T, LANES), 1)
        for hh in range(ATT_NH):
            q = src_ref[:, hh * LANES:(hh + 1) * LANES]
            zero = jnp.zeros_like(q)
            qq_scr[hh, 0:T] = jnp.where(lane < HEAD_DIM_A, q, zero)
            qq_scr[hh, T:2 * T] = jnp.where(lane >= HEAD_DIM_A, q, zero)
        m_scr[...] = jnp.full(m_scr.shape, -jnp.inf, F32)
        l_scr[...] = jnp.zeros(l_scr.shape, F32)
        acc_scr[...] = jnp.zeros(acc_scr.shape, F32)
        scores(0, sta_scr)

    @pl.when(qi == 0)
    def _():
        start_block(q_ref)

    def softmax_pv(t, st_ref, tile, heads=all_heads):
        k0 = pl.multiple_of(t * T, T)
        for hh in heads:
            st = st_ref[hh]
            if tile is not None:
                bias = bias_ref[hh, tile]
                st = st + jnp.concatenate([bias, bias], axis=1)
            m_prev = m_scr[hh]
            m_new = jnp.maximum(m_prev, jnp.max(st, axis=0, keepdims=True))
            alpha = jnp.exp2(m_prev - m_new)
            p = jnp.exp2(st - m_new).astype(BF16)
            vb = v_ref[pl.ds(k0, T), hh * LANES:(hh + 1) * LANES]
            l_scr[hh] = alpha * l_scr[hh] + jnp.dot(ones, p, preferred_element_type=F32)[0:1]
            pv = lax.dot_general(vb, p, (((0,), (0,)), ((), ())), preferred_element_type=F32)
            acc_scr[hh] = alpha * acc_scr[hh] + pv
            m_scr[hh] = m_new

    def tick(t, cur, nxt, tile):
        scores(t + 1, nxt, all_heads[:ATT_SKEW])
        for hh in all_heads:
            scores(t + 1, nxt, all_heads[hh + ATT_SKEW:hh + ATT_SKEW + 1])
            softmax_pv(t, cur, tile, (hh,))

    n_far = jnp.maximum(qi - 1, 0)

    def far_pair(i, carry):
        tick(2 * i, sta_scr, stb_scr, None)
        tick(2 * i + 1, stb_scr, sta_scr, None)
        return carry

    lax.fori_loop(0, n_far // 2, far_pair, 0)
    odd = lax.rem(n_far, 2) == 1

    @pl.when(odd)
    def _():
        tick(n_far - 1, sta_scr, stb_scr, None)

    @pl.when(jnp.logical_and(qi >= 1, jnp.logical_not(odd)))
    def _():
        tick(qi - 1, sta_scr, stb_scr, 1)
        softmax_pv(qi, stb_scr, 0)

    @pl.when(jnp.logical_and(qi >= 1, odd))
    def _():
        tick(qi - 1, stb_scr, sta_scr, 1)
        softmax_pv(qi, sta_scr, 0)

    @pl.when(qi == 0)
    def _():
        softmax_pv(0, sta_scr, 0)

    lv = lamv_ref[...]
    lam = (jnp.exp(jnp.sum(lv[0:1] * lv[1:2], axis=-1, keepdims=True))
           - jnp.exp(jnp.sum(lv[2:3] * lv[3:4], axis=-1, keepdims=True)) + lam_init)
    for hh in range(ATT_NH):
        ot = acc_scr[hh] / l_scr[hh]
        ot = ot[:, :T] - lam * ot[:, T:]
        ms = jnp.mean(ot * ot, axis=0, keepdims=True)
        on = (ot * lax.rsqrt(ms + EPS)).T * gs_ref[...] * (1.0 - lam_init)
        z = z_ref[:, hh * LANES:(hh + 1) * LANES].astype(F32)
        o_ref[:, hh * LANES:(hh + 1) * LANES] = (on * _silu(z)).astype(o_ref.dtype)

    start_block(qn_ref)


def _diff_attention(proj, bias_tiles, lamv, g_subln, B, S, lam_init):
    T = ATT_T
    nq = S // T
    H = N_HEADS_A
    d_a = H * 2 * HEAD_DIM_A
    W = ATT_NH * LANES
    G = H // ATT_NH
    return pl.pallas_call(
        functools.partial(_attn_kernel, lam_init=lam_init),
        grid=(B, G, nq),
        in_specs=[
            pl.BlockSpec((4, HEAD_DIM_A), lambda b, g, i: (0, 0)),
            pl.BlockSpec((1, LANES), lambda b, g, i: (0, 0)),
            pl.BlockSpec((T, W), lambda b, g, i: (b * nq + i, g)),
            pl.BlockSpec((T, W), lambda b, g, i: (b * nq + jnp.minimum(i + 1, nq - 1), g)),
            pl.BlockSpec((S, W), lambda b, g, i: (b, G + g)),
            pl.BlockSpec((S, W), lambda b, g, i: (b, 2 * G + g)),
            pl.BlockSpec((T, W), lambda b, g, i: (b * nq + i, 3 * G + g)),
            pl.BlockSpec((ATT_NH, 2, T, T), lambda b, g, i: (g, 0, 0, 0)),
        ],
        out_specs=pl.BlockSpec((T, W), lambda b, g, i: (b * nq + i, g)),
        out_shape=jax.ShapeDtypeStruct((B * S, d_a), BF16),
        scratch_shapes=[pltpu.VMEM((ATT_NH, 2 * T, LANES), BF16),
                        pltpu.VMEM((ATT_NH, T, 2 * T), F32),
                        pltpu.VMEM((ATT_NH, T, 2 * T), F32),
                        pltpu.VMEM((ATT_NH, 2 * HEAD_DIM_A, 2 * T), F32),
                        pltpu.VMEM((ATT_NH, 1, 2 * T), F32),
                        pltpu.VMEM((ATT_NH, 1, 2 * T), F32)],
        compiler_params=_cparams(("parallel", "parallel", "arbitrary")),
        name="diff_attn",
    )(lamv, g_subln, proj, proj, proj, proj, proj, bias_tiles)


def _conv_kernel(u_ref, halo_ref, z_ref, cw_ref, cb_ref, lng_ref, lnb_ref, o_ref, c_scr, y_scr):
    C = o_ref.shape[1]
    TS = o_ref.shape[0]

    def glu(u):
        return u[:, :C].astype(F32) * _sigmoid(u[:, C:].astype(F32))

    halo = glu(halo_ref[...])
    c_scr[0, 0:CONV_HALO] = jnp.where(pl.program_id(1) == 0, jnp.zeros_like(halo), halo)
    c_scr[0, CONV_HALO:CONV_HALO + TS] = glu(u_ref[...])
    c_scr[0, CONV_HALO + TS:] = jnp.zeros((SUBLANES, C), F32)

    def shift(r, carry):
        r0 = pl.multiple_of(r * CONV_RS, CONV_RS)
        win = c_scr[0, pl.ds(r0, CONV_RS + SUBLANES), :]
        for s in range(1, SUBLANES):
            c_scr[s, pl.ds(r0, CONV_RS), :] = win[s:s + CONV_RS]
        return carry

    lax.fori_loop(0, (CONV_HALO + TS) // CONV_RS, shift, 0)

    base = CONV_HALO - (CONV_K - 1)

    for lb in range(C // LANES):
        lanes = slice(lb * LANES, (lb + 1) * LANES)
        taps = [cw_ref[j, :, lanes] for j in range(CONV_K)]
        bias = cb_ref[:, lanes]

        def conv_rows(r, carry, lanes=lanes, taps=taps, bias=bias):
            r0 = pl.multiple_of(r * CONV_ROWS, CONV_ROWS)
            for g in range(CONV_ROWS // SUBLANES):
                acc = bias
                for j in range(CONV_K):
                    off = base + j
                    row = r0 + g * SUBLANES + off - off % SUBLANES
                    acc = acc + taps[j] * c_scr[off % SUBLANES, pl.ds(row, SUBLANES), lanes]
                y_scr[pl.ds(r0 + g * SUBLANES, SUBLANES), lanes] = acc
            return carry

        lax.fori_loop(0, TS // CONV_ROWS, conv_rows, 0)

    pack = 2 * SUBLANES

    def norm_rows(r, carry):
        r0 = pl.multiple_of(r * CONV_LN_ROWS, CONV_LN_ROWS)
        for s in range(CONV_LN_ROWS // pack):
            rs = r0 + s * pack
            z = z_ref[pl.ds(rs, pack), :].astype(F32)
            ys = []
            for g in range(pack // SUBLANES):
                acc = y_scr[pl.ds(rs + g * SUBLANES, SUBLANES), :]
                mu = jnp.mean(acc, axis=-1, keepdims=True)
                xc = acc - mu
                var = jnp.mean(xc * xc, axis=-1, keepdims=True)
                y = xc * lax.rsqrt(var + EPS) * lng_ref[...] + lnb_ref[...]
                ys.append(_silu(y) * _silu(z[g * SUBLANES:(g + 1) * SUBLANES]))
            o_ref[pl.ds(rs, pack), :] = jnp.concatenate(ys, axis=0).astype(o_ref.dtype)
        return carry

    lax.fori_loop(0, TS // CONV_LN_ROWS, norm_rows, 0)


def _conv_module(proj, conv_w, conv_b, ln_g, ln_b, B, S, u_off, z_off):
    C = conv_w.shape[-1]
    TS = CONV_TS
    ns = S // TS
    hb = TS // CONV_HALO
    return pl.pallas_call(
        _conv_kernel,
        grid=(B, ns),
        in_specs=[
            pl.BlockSpec((TS, 2 * C), lambda b, s: (b * ns + s, u_off // (2 * C))),
            pl.BlockSpec((CONV_HALO, 2 * C),
                         lambda b, s: (jnp.maximum((b * ns + s) * hb - 1, 0), u_off // (2 * C))),
            pl.BlockSpec((TS, C), lambda b, s: (b * ns + s, z_off // C)),
            pl.BlockSpec((CONV_K, SUBLANES, C), lambda b, s: (0, 0, 0)),
            pl.BlockSpec((SUBLANES, C), lambda b, s: (0, 0)),
            pl.BlockSpec((SUBLANES, C), lambda b, s: (0, 0)),
            pl.BlockSpec((SUBLANES, C), lambda b, s: (0, 0)),
        ],
        out_specs=pl.BlockSpec((TS, C), lambda b, s: (b * ns + s, 0)),
        out_shape=jax.ShapeDtypeStruct((B * S, C), BF16),
        scratch_shapes=[pltpu.VMEM((SUBLANES, CONV_HALO + TS + SUBLANES, C), F32),
                        pltpu.VMEM((TS, C), F32)],
        compiler_params=_cparams(("parallel", "parallel")),
        name="conv_module",
    )(proj, proj, proj, conv_w, conv_b, ln_g, ln_b)


def _memkv_kernel(mem_ref, g_ref, w_ref, o_ref):
    xf = mem_ref[...]
    ms = jnp.mean(xf * xf, axis=-1, keepdims=True)
    mn = (xf * lax.rsqrt(ms + EPS) * g_ref[...]).astype(BF16)
    o_ref[...] = jnp.dot(mn, w_ref[...], preferred_element_type=F32).astype(o_ref.dtype)


def _mem_kv(mem2, g_mem, w_mem_kv, n_mem):
    TM, D = mem2.shape
    N = w_mem_kv.shape[1]
    return pl.pallas_call(
        _memkv_kernel,
        grid=(TM // n_mem,),
        in_specs=[
            pl.BlockSpec((n_mem, D), lambda b: (b, 0)),
            pl.BlockSpec((1, D), lambda b: (0, 0)),
            pl.BlockSpec((D, N), lambda b: (0, 0)),
        ],
        out_specs=pl.BlockSpec((n_mem, N), lambda b: (b, 0)),
        out_shape=jax.ShapeDtypeStruct((TM, N), BF16),
        compiler_params=_cparams(("parallel",)),
        name="mem_kv",
    )(mem2, g_mem, w_mem_kv)


def _xattn_kernel(q_ref, z_ref, kv_ref, o_ref):
    dx = HEAD_DIM_X
    d_x = N_HEADS_X * dx
    for h in range(N_HEADS_X):
        qh = q_ref[:, h * dx:(h + 1) * dx]
        kh = kv_ref[:, h * dx:(h + 1) * dx]
        vh = kv_ref[:, d_x + h * dx:d_x + (h + 1) * dx]
        s = lax.dot_general(qh, kh, (((1,), (1,)), ((), ())), preferred_element_type=F32)
        m = jnp.max(s, axis=-1, keepdims=True)
        p = jnp.exp2(s - m)
        l = jnp.sum(p, axis=-1, keepdims=True)
        o = jnp.dot(p.astype(BF16), vh, preferred_element_type=F32) / l
        z = z_ref[:, h * dx:(h + 1) * dx].astype(F32)
        o_ref[:, h * dx:(h + 1) * dx] = (o * _silu(z)).astype(o_ref.dtype)


def _cross_attention(proj, kv, B, S, n_mem, q_off, z_off):
    d_x = N_HEADS_X * HEAD_DIM_X
    TQ = XATT_TQ
    nq = S // TQ
    return pl.pallas_call(
        _xattn_kernel,
        grid=(B, nq),
        in_specs=[
            pl.BlockSpec((TQ, d_x), lambda b, i: (b * nq + i, q_off // d_x)),
            pl.BlockSpec((TQ, d_x), lambda b, i: (b * nq + i, z_off // d_x)),
            pl.BlockSpec((n_mem, 2 * d_x), lambda b, i: (b, 0)),
        ],
        out_specs=pl.BlockSpec((TQ, d_x), lambda b, i: (b * nq + i, 0)),
        out_shape=jax.ShapeDtypeStruct((B * S, d_x), BF16),
        compiler_params=_cparams(("parallel", "parallel")),
        name="mem_xattn",
    )(proj, proj, kv)


def _merge_kernel(a_ref, b_ref, c_ref, gt_ref, x_ref, woa_ref, wob_ref, woc_ref, wout_ref,
                  bm_ref, gp_ref, o_ref):
    D = x_ref.shape[1]
    merged = jnp.zeros(x_ref.shape, F32)
    for n, (br_ref, w_ref) in enumerate(((a_ref, woa_ref), (b_ref, wob_ref), (c_ref, woc_ref))):
        yb = jnp.dot(br_ref[...], w_ref[...], preferred_element_type=F32)
        gate = _sigmoid(gt_ref[:, n * D:(n + 1) * D].astype(F32) + bm_ref[:, n * D:(n + 1) * D])
        merged = merged + gate * yb
    y = jnp.dot(merged.astype(BF16), wout_ref[...], preferred_element_type=F32)
    ms = jnp.mean(y * y, axis=-1, keepdims=True)
    o_ref[...] = x_ref[...] + y * lax.rsqrt(ms + EPS) * gp_ref[...]


def _merge(ya, yb, yc, proj, x2, w_oa, w_ob, w_oc, w_out, b_merge, g_post, g_off):
    T, D = x2.shape
    TM = MERGE_TM
    row = lambda i: (i, 0)
    fixed = lambda i: (0, 0)
    return pl.pallas_call(
        _merge_kernel,
        grid=(T // TM,),
        in_specs=[
            pl.BlockSpec((TM, D), row),
            pl.BlockSpec((TM, D), row),
            pl.BlockSpec((TM, D), row),
            pl.BlockSpec((TM, N_BRANCH * D), lambda i: (i, g_off // (N_BRANCH * D))),
            pl.BlockSpec((TM, D), row),
            pl.BlockSpec((D, D), fixed),
            pl.BlockSpec((D, D), fixed),
            pl.BlockSpec((D, D), fixed),
            pl.BlockSpec((D, D), fixed),
            pl.BlockSpec((1, N_BRANCH * D), fixed),
            pl.BlockSpec((1, D), fixed),
        ],
        out_specs=pl.BlockSpec((TM, D), row),
        out_shape=jax.ShapeDtypeStruct((T, D), F32),
        compiler_params=_cparams(("parallel",)),
        name="merge_out",
    )(ya, yb, yc, proj, x2, w_oa, w_ob, w_oc, w_out, b_merge, g_post)


def kernel(x, mem, rel_bias, g_pre, g_mem, w_in, b_merge, lam_q1, lam_k1, lam_q2, lam_k2,
           g_subln, w_oa, conv_w, conv_b, ln_g, ln_b, w_ob, w_mem_kv, w_oc, w_out, g_post):
    B, S, D = x.shape
    n_mem = mem.shape[1]
    depth = w_in.shape[0]
    d_a = N_HEADS_A * 2 * HEAD_DIM_A
    d_conv = conv_w.shape[2]
    d_x = N_HEADS_X * HEAD_DIM_X
    sizes = (d_a, d_a, d_a, d_a, 2 * d_conv, d_conv, d_x, d_x, N_BRANCH * D)
    offs = [int(o) for o in np.cumsum((0,) + sizes)]
    assert offs[-1] == w_in.shape[2] and d_a == d_conv == d_x == D
    assert MAX_DIST <= ATT_T and ATT_T % CHUNK == 0 and CONV_HALO >= CONV_K - 1
    assert (B * S) % PROJ_TM == 0 and offs[-1] % PROJ_TN == 0 and (B * S) % MERGE_TM == 0
    assert S % ATT_T == 0 and N_HEADS_A % ATT_NH == 0 and S % CONV_TS == 0 and S % XATT_TQ == 0
    assert CONV_TS % CONV_LN_ROWS == 0 and CONV_TS % CONV_ROWS == 0 and CONV_TS % CONV_HALO == 0
    assert (CONV_HALO + CONV_TS) % CONV_RS == 0

    x2 = x.reshape(B * S, D)
    mem2 = mem.reshape(B * n_mem, D)
    bias_tiles = _bias_tiles(rel_bias)
    col_scale = (jnp.ones((SUBLANES, offs[-1]), F32).at[:, :d_a].set(Q_SCALE_LOG2)
                 .at[:, offs[6]:offs[7]].set(XQ_SCALE_LOG2))
    for l in range(depth):
        lam_init = 0.8 - 0.6 * math.exp(-0.3 * l)
        proj = _proj(x2, g_pre[l][None], w_in[l].astype(BF16), col_scale)

        lamv = jnp.stack([lam_q1[l], lam_k1[l], lam_q2[l], lam_k2[l]])
        ya = _diff_attention(proj, bias_tiles, lamv, g_subln[l][None], B, S, lam_init)

        rep = lambda v: jnp.broadcast_to(v[..., None, :], v.shape[:-1] + (SUBLANES, v.shape[-1]))
        yb = _conv_module(proj, rep(conv_w[l]), rep(conv_b[l]), rep(ln_g[l]), rep(ln_b[l]),
                          B, S, offs[4], offs[5])

        kv = _mem_kv(mem2, g_mem[l][None], w_mem_kv[l].astype(BF16), n_mem)
        yc = _cross_attention(proj, kv, B, S, n_mem, offs[6], offs[7])

        x2 = _merge(ya, yb, yc, proj, x2, w_oa[l].astype(BF16), w_ob[l].astype(BF16),
                    w_oc[l].astype(BF16), w_out[l].astype(BF16), b_merge[l][None], g_post[l][None],
                    offs[8])
    return x2.reshape(B, S, D)
```

```python
import functools
import math

import jax
import jax.numpy as jnp
import numpy as np
from jax import lax
from jax.experimental import pallas as pl
from jax.experimental.pallas import tpu as pltpu

CHUNK = 64
N_HEADS_A = 8
HEAD_DIM_A = 64
N_HEADS_X = 4
HEAD_DIM_X = 256
CONV_K = 31
N_BRANCH = 3
N_BUCKETS = 32
MAX_DIST = 128
EPS = 1e-6
NEG_INF = -1e30
LOG2_E = math.log2(math.e)
Q_SCALE_LOG2 = HEAD_DIM_A ** -0.5 * LOG2_E
XQ_SCALE_LOG2 = HEAD_DIM_X ** -0.5 * LOG2_E

LANES = 128
SUBLANES = 8
VMEM_LIMIT_BYTES = 48 * 1024 * 1024

PROJ_TM = 1024
PROJ_TN = 4096
ATT_T = 256
ATT_NH = 8
ATT_SKEW = 1
CONV_TS = 512
CONV_HALO = 32
CONV_RS = 32
CONV_LN_ROWS = 128
CONV_ROWS = 64
XATT_TQ = 512
MERGE_TM = 512

BF16 = jnp.bfloat16
F32 = jnp.float32


def _cparams(sem):
    return pltpu.CompilerParams(dimension_semantics=sem, vmem_limit_bytes=VMEM_LIMIT_BYTES)


def _sigmoid(z):
    return 0.5 * jnp.tanh(0.5 * z) + 0.5


def _silu(z):
    return z * _sigmoid(z)


def _proj_kernel(x_ref, g_ref, w_ref, cs_ref, o_ref, h_scr):
    @pl.when(pl.program_id(1) == 0)
    def _():
        xf = x_ref[...]
        ms = jnp.mean(xf * xf, axis=-1, keepdims=True)
        h_scr[...] = (xf * lax.rsqrt(ms + EPS) * g_ref[...]).astype(BF16)

    acc = jnp.dot(h_scr[...], w_ref[...], preferred_element_type=F32)
    tm, tn = acc.shape
    acc = acc.reshape(tm // SUBLANES, SUBLANES, tn) * cs_ref[...][None]
    o_ref[...] = acc.reshape(tm, tn).astype(o_ref.dtype)


def _proj(x2, g_pre, w_in, col_scale):
    T, D = x2.shape
    N = w_in.shape[1]
    return pl.pallas_call(
        _proj_kernel,
        grid=(T // PROJ_TM, N // PROJ_TN),
        in_specs=[
            pl.BlockSpec((PROJ_TM, D), lambda i, j: (i, 0)),
            pl.BlockSpec((1, D), lambda i, j: (0, 0)),
            pl.BlockSpec((D, PROJ_TN), lambda i, j: (0, j)),
            pl.BlockSpec((SUBLANES, PROJ_TN), lambda i, j: (0, j)),
        ],
        out_specs=pl.BlockSpec((PROJ_TM, PROJ_TN), lambda i, j: (i, j)),
        out_shape=jax.ShapeDtypeStruct((T, N), BF16),
        scratch_shapes=[pltpu.VMEM((PROJ_TM, D), BF16)],
        compiler_params=_cparams(("parallel", "arbitrary")),
        name="proj",
    )(x2, g_pre, w_in, col_scale)


def _t5_bucket(rel):
    nb = N_BUCKETS // 2
    max_exact = nb // 2
    ret = (rel > 0).astype(jnp.int32) * nb
    n = jnp.abs(rel)
    nf = jnp.maximum(n, 1).astype(jnp.float32)
    large = max_exact + (jnp.log(nf / max_exact) / math.log(MAX_DIST / max_exact)
                         * (nb - max_exact)).astype(jnp.int32)
    large = jnp.minimum(large, nb - 1)
    return ret + jnp.where(n < max_exact, n, large)


def _bucket_tiles():
    kpos = jnp.arange(ATT_T, dtype=jnp.int32)[:, None]
    qpos = jnp.arange(ATT_T, dtype=jnp.int32)[None, :]
    tiles = []
    for d in (0, -1, -2):
        rel = kpos + d * ATT_T - qpos
        tiles.append(_t5_bucket(rel))
    allowed = (kpos // CHUNK) <= (qpos // CHUNK)
    tiles[0] = jnp.where(allowed, tiles[0], -1)
    return jnp.stack(tiles)


def _bias_kernel(relb_ref, bkt_ref, o_ref):
    h = pl.program_id(0)
    bkt = bkt_ref[...]
    res = jnp.zeros(bkt.shape, F32)
    for b in range(N_BUCKETS):
        res = jnp.where(bkt == b, relb_ref[b, h], res)
    near = res[0:2] - res[2, 0:1, 0:1]
    o_ref[0] = jnp.where(bkt[0:2] < 0, NEG_INF, near * LOG2_E)


def _bias_tiles(rel_bias):
    bkt = _bucket_tiles()
    return pl.pallas_call(
        _bias_kernel,
        grid=(N_HEADS_A,),
        in_specs=[
            pl.BlockSpec(memory_space=pltpu.SMEM),
            pl.BlockSpec((3, ATT_T, ATT_T), lambda h: (0, 0, 0)),
        ],
        out_specs=pl.BlockSpec((1, 2, ATT_T, ATT_T), lambda h: (h, 0, 0, 0)),
        out_shape=jax.ShapeDtypeStruct((N_HEADS_A, 2, ATT_T, ATT_T), F32),
        compiler_params=_cparams(("arbitrary",)),
        name="bias_tiles",
    )(rel_bias, bkt)


def _attn_kernel(lamv_ref, gs_ref, q_ref, qn_ref, k_ref, v_ref, z_ref, bias_ref, o_ref,
                 qq_scr, sta_scr, stb_scr, acc_scr, m_scr, l_scr, *, lam_init):
    T = ATT_T
    qi = pl.program_id(2)
    ones = jnp.ones((SUBLANES, T), BF16)

    all_heads = tuple(range(ATT_NH))

    def scores(t, st_ref, heads=all_heads):
        k0 = pl.multiple_of(t * T, T)
        for hh in heads:
            kb = k_ref[pl.ds(k0, T), hh * LANES:(hh + 1) * LANES]
            st_ref[hh] = jnp.dot(kb, qq_scr[hh], preferred_element_type=F32)

    def start_block(src_ref):
        lane = lax.broadcasted_iota(jnp.int32, (T, LANES), 1)
        for hh in range(ATT_NH):
            q = src_ref[:, hh * LANES:(hh + 1) * LANES]
            zero = jnp.zeros_like(q)
            qq_scr[hh, :, 0:T] = jnp.where(lane < HEAD_DIM_A, q, zero).T
            qq_scr[hh, :, T:2 * T] = jnp.where(lane >= HEAD_DIM_A, q, zero).T
        m_scr[...] = jnp.full(m_scr.shape, -jnp.inf, F32)
        l_scr[...] = jnp.zeros(l_scr.shape, F32)
        acc_scr[...] = jnp.zeros(acc_scr.shape, F32)
        scores(0, sta_scr)

    @pl.when(qi == 0)
    def _():
        start_block(q_ref)

    def softmax_pv(t, st_ref, tile, heads=all_heads):
        k0 = pl.multiple_of(t * T, T)
        for hh in heads:
            st = st_ref[hh]
            if tile is not None:
                bias = bias_ref[hh, tile]
                st = st + jnp.concatenate([bias, bias], axis=1)
            m_prev = m_scr[hh]
            m_new = jnp.maximum(m_prev, jnp.max(st, axis=0, keepdims=True))
            alpha = jnp.exp2(m_prev - m_new)
            p = jnp.exp2(st - m_new).astype(BF16)
            vb = v_ref[pl.ds(k0, T), hh * LANES:(hh + 1) * LANES]
            l_scr[hh] = alpha * l_scr[hh] + jnp.dot(ones, p, preferred_element_type=F32)[0:1]
            pv = lax.dot_general(vb, p, (((0,), (0,)), ((), ())), preferred_element_type=F32)
            acc_scr[hh] = alpha * acc_scr[hh] + pv
            m_scr[hh] = m_new

    def tick(t, cur, nxt, tile):
        scores(t + 1, nxt, all_heads[:ATT_SKEW])
        for hh in all_heads:
            scores(t + 1, nxt, all_heads[hh + ATT_SKEW:hh + ATT_SKEW + 1])
            softmax_pv(t, cur, tile, (hh,))

    n_far = jnp.maximum(qi - 1, 0)

    def far_pair(i, carry):
        tick(2 * i, sta_scr, stb_scr, None)
        tick(2 * i + 1, stb_scr, sta_scr, None)
        return carry

    lax.fori_loop(0, n_far // 2, far_pair, 0)
    odd = lax.rem(n_far, 2) == 1

    @pl.when(odd)
    def _():
        tick(n_far - 1, sta_scr, stb_scr, None)

    @pl.when(jnp.logical_and(qi >= 1, jnp.logical_not(odd)))
    def _():
        tick(qi - 1, sta_scr, stb_scr, 1)
        softmax_pv(qi, stb_scr, 0)

    @pl.when(jnp.logical_and(qi >= 1, odd))
    def _():
        tick(qi - 1, stb_scr, sta_scr, 1)
        softmax_pv(qi, sta_scr, 0)

    @pl.when(qi == 0)
    def _():
        softmax_pv(0, sta_scr, 0)

    lv = lamv_ref[...]
    lam = (jnp.exp(jnp.sum(lv[0:1] * lv[1:2], axis=-1, keepdims=True))
           - jnp.exp(jnp.sum(lv[2:3] * lv[3:4], axis=-1, keepdims=True)) + lam_init)
    for hh in range(ATT_NH):
        ot = acc_scr[hh] / l_scr[hh]
        ot = ot[:, :T] - lam * ot[:, T:]
        ms = jnp.mean(ot * ot, axis=0, keepdims=True)
        on = (ot * lax.rsqrt(ms + EPS)).T * gs_ref[...] * (1.0 - lam_init)
        z = z_ref[:, hh * LANES:(hh + 1) * LANES].astype(F32)
        o_ref[:, hh * LANES:(hh + 1) * LANES] = (on * _silu(z)).astype(o_ref.dtype)

    start_block(qn_ref)


def _diff_attention(proj, bias_tiles, lamv, g_subln, B, S, lam_init):
    T = ATT_T
    nq = S // T
    H = N_HEADS_A
    d_a = H * 2 * HEAD_DIM_A
    W = ATT_NH * LANES
    G = H // ATT_NH
    return pl.pallas_call(
        functools.partial(_attn_kernel, lam_init=lam_init),
        grid=(B, G, nq),
        in_specs=[
            pl.BlockSpec((4, HEAD_DIM_A), lambda b, g, i: (0, 0)),
            pl.BlockSpec((1, LANES), lambda b, g, i: (0, 0)),
            pl.BlockSpec((T, W), lambda b, g, i: (b * nq + i, g)),
            pl.BlockSpec((T, W), lambda b, g, i: (b * nq + jnp.minimum(i + 1, nq - 1), g)),
            pl.BlockSpec((S, W), lambda b, g, i: (b, G + g)),
            pl.BlockSpec((S, W), lambda b, g, i: (b, 2 * G + g)),
            pl.BlockSpec((T, W), lambda b, g, i: (b * nq + i, 3 * G + g)),
            pl.BlockSpec((ATT_NH, 2, T, T), lambda b, g, i: (g, 0, 0, 0)),
        ],
        out_specs=pl.BlockSpec((T, W), lambda b, g, i: (b * nq + i, g)),
        out_shape=jax.ShapeDtypeStruct((B * S, d_a), BF16),
        scratch_shapes=[pltpu.VMEM((ATT_NH, LANES, 2 * T), BF16),
                        pltpu.VMEM((ATT_NH, T, 2 * T), F32),
                        pltpu.VMEM((ATT_NH, T, 2 * T), F32),
                        pltpu.VMEM((ATT_NH, 2 * HEAD_DIM_A, 2 * T), F32),
                        pltpu.VMEM((ATT_NH, 1, 2 * T), F32),
                        pltpu.VMEM((ATT_NH, 1, 2 * T), F32)],
        compiler_params=_cparams(("parallel", "parallel", "arbitrary")),
        name="diff_attn",
    )(lamv, g_subln, proj, proj, proj, proj, proj, bias_tiles)


def _conv_kernel(u_ref, halo_ref, z_ref, cw_ref, cb_ref, lng_ref, lnb_ref, o_ref, c_scr, y_scr):
    C = o_ref.shape[1]
    TS = o_ref.shape[0]

    def glu(u):
        return u[:, :C].astype(F32) * _sigmoid(u[:, C:].astype(F32))

    halo = glu(halo_ref[...])
    c_scr[0, 0:CONV_HALO] = jnp.where(pl.program_id(1) == 0, jnp.zeros_like(halo), halo)
    c_scr[0, CONV_HALO:CONV_HALO + TS] = glu(u_ref[...])
    c_scr[0, CONV_HALO + TS:] = jnp.zeros((SUBLANES, C), F32)

    def shift(r, carry):
        r0 = pl.multiple_of(r * CONV_RS, CONV_RS)
        win = c_scr[0, pl.ds(r0, CONV_RS + SUBLANES), :]
        for s in range(1, SUBLANES):
            c_scr[s, pl.ds(r0, CONV_RS), :] = win[s:s + CONV_RS]
        return carry

    lax.fori_loop(0, (CONV_HALO + TS) // CONV_RS, shift, 0)

    base = CONV_HALO - (CONV_K - 1)

    for lb in range(C // LANES):
        lanes = slice(lb * LANES, (lb + 1) * LANES)
        taps = [cw_ref[j, :, lanes] for j in range(CONV_K)]
        bias = cb_ref[:, lanes]

        def conv_rows(r, carry, lanes=lanes, taps=taps, bias=bias):
            r0 = pl.multiple_of(r * CONV_ROWS, CONV_ROWS)
            for g in range(CONV_ROWS // SUBLANES):
                acc = bias
                for j in range(CONV_K):
                    off = base + j
                    row = r0 + g * SUBLANES + off - off % SUBLANES
                    acc = acc + taps[j] * c_scr[off % SUBLANES, pl.ds(row, SUBLANES), lanes]
                y_scr[pl.ds(r0 + g * SUBLANES, SUBLANES), lanes] = acc
            return carry

        lax.fori_loop(0, TS // CONV_ROWS, conv_rows, 0)

    pack = 2 * SUBLANES

    def norm_rows(r, carry):
        r0 = pl.multiple_of(r * CONV_LN_ROWS, CONV_LN_ROWS)
        for s in range(CONV_LN_ROWS // pack):
            rs = r0 + s * pack
            z = z_ref[pl.ds(rs, pack), :].astype(F32)
            ys = []
            for g in range(pack // SUBLANES):
                acc = y_scr[pl.ds(rs + g * SUBLANES, SUBLANES), :]
                mu = jnp.mean(acc, axis=-1, keepdims=True)
                xc = acc - mu
                var = jnp.mean(xc * xc, axis=-1, keepdims=True)
                y = xc * lax.rsqrt(var + EPS) * lng_ref[...] + lnb_ref[...]
                ys.append(_silu(y) * _silu(z[g * SUBLANES:(g + 1) * SUBLANES]))
            o_ref[pl.ds(rs, pack), :] = jnp.concatenate(ys, axis=0).astype(o_ref.dtype)
        return carry

    lax.fori_loop(0, TS // CONV_LN_ROWS, norm_rows, 0)


def _conv_module(proj, conv_w, conv_b, ln_g, ln_b, B, S, u_off, z_off):
    C = conv_w.shape[-1]
    TS = CONV_TS
    ns = S // TS
    hb = TS // CONV_HALO
    return pl.pallas_call(
        _conv_kernel,
        grid=(B, ns),
        in_specs=[
            pl.BlockSpec((TS, 2 * C), lambda b, s: (b * ns + s, u_off // (2 * C))),
            pl.BlockSpec((CONV_HALO, 2 * C),
                         lambda b, s: (jnp.maximum((b * ns + s) * hb - 1, 0), u_off // (2 * C))),
            pl.BlockSpec((TS, C), lambda b, s: (b * ns + s, z_off // C)),
            pl.BlockSpec((CONV_K, SUBLANES, C), lambda b, s: (0, 0, 0)),
            pl.BlockSpec((SUBLANES, C), lambda b, s: (0, 0)),
            pl.BlockSpec((SUBLANES, C), lambda b, s: (0, 0)),
            pl.BlockSpec((SUBLANES, C), lambda b, s: (0, 0)),
        ],
        out_specs=pl.BlockSpec((TS, C), lambda b, s: (b * ns + s, 0)),
        out_shape=jax.ShapeDtypeStruct((B * S, C), BF16),
        scratch_shapes=[pltpu.VMEM((SUBLANES, CONV_HALO + TS + SUBLANES, C), F32),
                        pltpu.VMEM((TS, C), F32)],
        compiler_params=_cparams(("parallel", "parallel")),
        name="conv_module",
    )(proj, proj, proj, conv_w, conv_b, ln_g, ln_b)


def _memkv_kernel(mem_ref, g_ref, w_ref, o_ref):
    xf = mem_ref[...]
    ms = jnp.mean(xf * xf, axis=-1, keepdims=True)
    mn = (xf * lax.rsqrt(ms + EPS) * g_ref[...]).astype(BF16)
    o_ref[...] = jnp.dot(mn, w_ref[...], preferred_element_type=F32).astype(o_ref.dtype)


def _mem_kv(mem2, g_mem, w_mem_kv, n_mem):
    TM, D = mem2.shape
    N = w_mem_kv.shape[1]
    return pl.pallas_call(
        _memkv_kernel,
        grid=(TM // n_mem,),
        in_specs=[
            pl.BlockSpec((n_mem, D), lambda b: (b, 0)),
            pl.BlockSpec((1, D), lambda b: (0, 0)),
            pl.BlockSpec((D, N), lambda b: (0, 0)),
        ],
        out_specs=pl.BlockSpec((n_mem, N), lambda b: (b, 0)),
        out_shape=jax.ShapeDtypeStruct((TM, N), BF16),
        compiler_params=_cparams(("parallel",)),
        name="mem_kv",
    )(mem2, g_mem, w_mem_kv)


def _xattn_kernel(q_ref, z_ref, kv_ref, o_ref):
    dx = HEAD_DIM_X
    d_x = N_HEADS_X * dx
    for h in range(N_HEADS_X):
        qh = q_ref[:, h * dx:(h + 1) * dx]
        kh = kv_ref[:, h * dx:(h + 1) * dx]
        vh = kv_ref[:, d_x + h * dx:d_x + (h + 1) * dx]
        s = lax.dot_general(qh, kh, (((1,), (1,)), ((), ())), preferred_element_type=F32)
        m = jnp.max(s, axis=-1, keepdims=True)
        p = jnp.exp2(s - m)
        l = jnp.sum(p, axis=-1, keepdims=True)
        o = jnp.dot(p.astype(BF16), vh, preferred_element_type=F32) / l
        z = z_ref[:, h * dx:(h + 1) * dx].astype(F32)
        o_ref[:, h * dx:(h + 1) * dx] = (o * _silu(z)).astype(o_ref.dtype)


def _cross_attention(proj, kv, B, S, n_mem, q_off, z_off):
    d_x = N_HEADS_X * HEAD_DIM_X
    TQ = XATT_TQ
    nq = S // TQ
    return pl.pallas_call(
        _xattn_kernel,
        grid=(B, nq),
        in_specs=[
            pl.BlockSpec((TQ, d_x), lambda b, i: (b * nq + i, q_off // d_x)),
            pl.BlockSpec((TQ, d_x), lambda b, i: (b * nq + i, z_off // d_x)),
            pl.BlockSpec((n_mem, 2 * d_x), lambda b, i: (b, 0)),
        ],
        out_specs=pl.BlockSpec((TQ, d_x), lambda b, i: (b * nq + i, 0)),
        out_shape=jax.ShapeDtypeStruct((B * S, d_x), BF16),
        compiler_params=_cparams(("parallel", "parallel")),
        name="mem_xattn",
    )(proj, proj, kv)


def _merge_kernel(a_ref, b_ref, c_ref, gt_ref, x_ref, woa_ref, wob_ref, woc_ref, wout_ref,
                  bm_ref, gp_ref, o_ref):
    D = x_ref.shape[1]
    merged = jnp.zeros(x_ref.shape, F32)
    for n, (br_ref, w_ref) in enumerate(((a_ref, woa_ref), (b_ref, wob_ref), (c_ref, woc_ref))):
        yb = jnp.dot(br_ref[...], w_ref[...], preferred_element_type=F32)
        gate = _sigmoid(gt_ref[:, n * D:(n + 1) * D].astype(F32) + bm_ref[:, n * D:(n + 1) * D])
        merged = merged + gate * yb
    y = jnp.dot(merged.astype(BF16), wout_ref[...], preferred_element_type=F32)
    ms = jnp.mean(y * y, axis=-1, keepdims=True)
    o_ref[...] = x_ref[...] + y * lax.rsqrt(ms + EPS) * gp_ref[...]


def _merge(ya, yb, yc, proj, x2, w_oa, w_ob, w_oc, w_out, b_merge, g_post, g_off):
    T, D = x2.shape
    TM = MERGE_TM
    row = lambda i: (i, 0)
    fixed = lambda i: (0, 0)
    return pl.pallas_call(
        _merge_kernel,
        grid=(T // TM,),
        in_specs=[
            pl.BlockSpec((TM, D), row),
            pl.BlockSpec((TM, D), row),
            pl.BlockSpec((TM, D), row),
            pl.BlockSpec((TM, N_BRANCH * D), lambda i: (i, g_off // (N_BRANCH * D))),
            pl.BlockSpec((TM, D), row),
            pl.BlockSpec((D, D), fixed),
            pl.BlockSpec((D, D), fixed),
            pl.BlockSpec((D, D), fixed),
            pl.BlockSpec((D, D), fixed),
            pl.BlockSpec((1, N_BRANCH * D), fixed),
            pl.BlockSpec((1, D), fixed),
        ],
        out_specs=pl.BlockSpec((TM, D), row),
        out_shape=jax.ShapeDtypeStruct((T, D), F32),
        compiler_params=_cparams(("parallel",)),
        name="merge_out",
    )(ya, yb, yc, proj, x2, w_oa, w_ob, w_oc, w_out, b_merge, g_post)


def kernel(x, mem, rel_bias, g_pre, g_mem, w_in, b_merge, lam_q1, lam_k1, lam_q2, lam_k2,
           g_subln, w_oa, conv_w, conv_b, ln_g, ln_b, w_ob, w_mem_kv, w_oc, w_out, g_post):
    B, S, D = x.shape
    n_mem = mem.shape[1]
    depth = w_in.shape[0]
    d_a = N_HEADS_A * 2 * HEAD_DIM_A
    d_conv = conv_w.shape[2]
    d_x = N_HEADS_X * HEAD_DIM_X
    sizes = (d_a, d_a, d_a, d_a, 2 * d_conv, d_conv, d_x, d_x, N_BRANCH * D)
    offs = [int(o) for o in np.cumsum((0,) + sizes)]
    assert offs[-1] == w_in.shape[2] and d_a == d_conv == d_x == D
    assert MAX_DIST <= ATT_T and ATT_T % CHUNK == 0 and CONV_HALO >= CONV_K - 1
    assert (B * S) % PROJ_TM == 0 and offs[-1] % PROJ_TN == 0 and (B * S) % MERGE_TM == 0
    assert S % ATT_T == 0 and N_HEADS_A % ATT_NH == 0 and S % CONV_TS == 0 and S % XATT_TQ == 0
    assert CONV_TS % CONV_LN_ROWS == 0 and CONV_TS % CONV_ROWS == 0 and CONV_TS % CONV_HALO == 0
    assert (CONV_HALO + CONV_TS) % CONV_RS == 0

    x2 = x.reshape(B * S, D)
    mem2 = mem.reshape(B * n_mem, D)
    bias_tiles = _bias_tiles(rel_bias)
    col_scale = (jnp.ones((SUBLANES, offs[-1]), F32).at[:, :d_a].set(Q_SCALE_LOG2)
                 .at[:, offs[6]:offs[7]].set(XQ_SCALE_LOG2))
    for l in range(depth):
        lam_init = 0.8 - 0.6 * math.exp(-0.3 * l)
        proj = _proj(x2, g_pre[l][None], w_in[l].astype(BF16), col_scale)

        lamv = jnp.stack([lam_q1[l], lam_k1[l], lam_q2[l], lam_k2[l]])
        ya = _diff_attention(proj, bias_tiles, lamv, g_subln[l][None], B, S, lam_init)

        rep = lambda v: jnp.broadcast_to(v[..., None, :], v.shape[:-1] + (SUBLANES, v.shape[-1]))
        yb = _conv_module(proj, rep(conv_w[l]), rep(conv_b[l]), rep(ln_g[l]), rep(ln_b[l]),
                          B, S, offs[4], offs[5])

        kv = _mem_kv(mem2, g_mem[l][None], w_mem_kv[l].astype(BF16), n_mem)
        yc = _cross_attention(proj, kv, B, S, n_mem, offs[6], offs[7])

        x2 = _merge(ya, yb, yc, proj, x2, w_oa[l].astype(BF16), w_ob[l].astype(BF16),
                    w_oc[l].astype(BF16), w_out[l].astype(BF16), b_merge[l][None], g_post[l][None],
                    offs[8])
    return x2.reshape(B, S, D)
```

```python
import functools
import math

import jax
import jax.numpy as jnp
import numpy as np
from jax import lax
from jax.experimental import pallas as pl
from jax.experimental.pallas import tpu as pltpu

CHUNK = 64
N_HEADS_A = 8
HEAD_DIM_A = 64
N_HEADS_X = 4
HEAD_DIM_X = 256
CONV_K = 31
N_BRANCH = 3
N_BUCKETS = 32
MAX_DIST = 128
EPS = 1e-6
NEG_INF = -1e30
LOG2_E = math.log2(math.e)
Q_SCALE_LOG2 = HEAD_DIM_A ** -0.5 * LOG2_E
XQ_SCALE_LOG2 = HEAD_DIM_X ** -0.5 * LOG2_E

LANES = 128
SUBLANES = 8
VMEM_LIMIT_BYTES = 48 * 1024 * 1024

PROJ_TM = 1024
PROJ_TN = 4096
ATT_T = 256
ATT_NH = 8
ATT_SKEW = 1
CONV_TS = 512
CONV_HALO = 32
CONV_RS = 32
CONV_LN_ROWS = 128
CONV_ROWS = 64
XATT_TQ = 512
MERGE_TM = 512

BF16 = jnp.bfloat16
F32 = jnp.float32


def _cparams(sem):
    return pltpu.CompilerParams(dimension_semantics=sem, vmem_limit_bytes=VMEM_LIMIT_BYTES)


def _sigmoid(z):
    return 0.5 * jnp.tanh(0.5 * z) + 0.5


def _silu(z):
    return z * _sigmoid(z)


def _proj_kernel(x_ref, g_ref, w_ref, cs_ref, o_ref, h_scr):
    @pl.when(pl.program_id(1) == 0)
    def _():
        xf = x_ref[...]
        ms = jnp.mean(xf * xf, axis=-1, keepdims=True)
        h_scr[...] = (xf * lax.rsqrt(ms + EPS) * g_ref[...]).astype(BF16)

    acc = jnp.dot(h_scr[...], w_ref[...], preferred_element_type=F32)
    tm, tn = acc.shape
    acc = acc.reshape(tm // SUBLANES, SUBLANES, tn) * cs_ref[...][None]
    o_ref[...] = acc.reshape(tm, tn).astype(o_ref.dtype)


def _proj(x2, g_pre, w_in, col_scale):
    T, D = x2.shape
    N = w_in.shape[1]
    return pl.pallas_call(
        _proj_kernel,
        grid=(T // PROJ_TM, N // PROJ_TN),
        in_specs=[
            pl.BlockSpec((PROJ_TM, D), lambda i, j: (i, 0)),
            pl.BlockSpec((1, D), lambda i, j: (0, 0)),
            pl.BlockSpec((D, PROJ_TN), lambda i, j: (0, j)),
            pl.BlockSpec((SUBLANES, PROJ_TN), lambda i, j: (0, j)),
        ],
        out_specs=pl.BlockSpec((PROJ_TM, PROJ_TN), lambda i, j: (i, j)),
        out_shape=jax.ShapeDtypeStruct((T, N), BF16),
        scratch_shapes=[pltpu.VMEM((PROJ_TM, D), BF16)],
        compiler_params=_cparams(("parallel", "arbitrary")),
        name="proj",
    )(x2, g_pre, w_in, col_scale)


def _t5_bucket(rel):
    nb = N_BUCKETS // 2
    max_exact = nb // 2
    ret = (rel > 0).astype(jnp.int32) * nb
    n = jnp.abs(rel)
    nf = jnp.maximum(n, 1).astype(jnp.float32)
    large = max_exact + (jnp.log(nf / max_exact) / math.log(MAX_DIST / max_exact)
                         * (nb - max_exact)).astype(jnp.int32)
    large = jnp.minimum(large, nb - 1)
    return ret + jnp.where(n < max_exact, n, large)


def _bucket_tiles():
    kpos = jnp.arange(ATT_T, dtype=jnp.int32)[:, None]
    qpos = jnp.arange(ATT_T, dtype=jnp.int32)[None, :]
    tiles = []
    for d in (0, -1, -2):
        rel = kpos + d * ATT_T - qpos
        tiles.append(_t5_bucket(rel))
    allowed = (kpos // CHUNK) <= (qpos // CHUNK)
    tiles[0] = jnp.where(allowed, tiles[0], -1)
    return jnp.stack(tiles)


def _bias_kernel(relb_ref, bkt_ref, o_ref):
    h = pl.program_id(0)
    bkt = bkt_ref[...]
    res = jnp.zeros(bkt.shape, F32)
    for b in range(N_BUCKETS):
        res = jnp.where(bkt == b, relb_ref[b, h], res)
    near = res[0:2] - res[2, 0:1, 0:1]
    o_ref[0] = jnp.where(bkt[0:2] < 0, NEG_INF, near * LOG2_E)


def _bias_tiles(rel_bias):
    bkt = _bucket_tiles()
    return pl.pallas_call(
        _bias_kernel,
        grid=(N_HEADS_A,),
        in_specs=[
            pl.BlockSpec(memory_space=pltpu.SMEM),
            pl.BlockSpec((3, ATT_T, ATT_T), lambda h: (0, 0, 0)),
        ],
        out_specs=pl.BlockSpec((1, 2, ATT_T, ATT_T), lambda h: (h, 0, 0, 0)),
        out_shape=jax.ShapeDtypeStruct((N_HEADS_A, 2, ATT_T, ATT_T), F32),
        compiler_params=_cparams(("arbitrary",)),
        name="bias_tiles",
    )(rel_bias, bkt)


def _attn_kernel(lamv_ref, gs_ref, q_ref, qn_ref, k_ref, v_ref, z_ref, bias_ref, o_ref,
                 qq_scr, sta_scr, stb_scr, acc_scr, m_scr, l_scr, *, lam_init):
    T = ATT_T
    qi = pl.program_id(2)
    ones = jnp.ones((2 * SUBLANES, T), BF16)

    all_heads = tuple(range(ATT_NH))

    def scores(t, st_ref, heads=all_heads):
        k0 = pl.multiple_of(t * T, T)
        for hh in heads:
            kb = k_ref[pl.ds(k0, T), hh * LANES:(hh + 1) * LANES]
            st_ref[hh] = jnp.dot(kb, qq_scr[hh], preferred_element_type=F32)

    def start_block(src_ref):
        lane = lax.broadcasted_iota(jnp.int32, (T, LANES), 1)
        for hh in range(ATT_NH):
            q = src_ref[:, hh * LANES:(hh + 1) * LANES]
            zero = jnp.zeros_like(q)
            qq_scr[hh, :, 0:T] = jnp.where(lane < HEAD_DIM_A, q, zero).T
            qq_scr[hh, :, T:2 * T] = jnp.where(lane >= HEAD_DIM_A, q, zero).T
        m_scr[...] = jnp.full(m_scr.shape, -jnp.inf, F32)
        l_scr[...] = jnp.zeros(l_scr.shape, F32)
        acc_scr[...] = jnp.zeros(acc_scr.shape, F32)
        scores(0, sta_scr)

    @pl.when(qi == 0)
    def _():
        start_block(q_ref)

    def softmax_pv(t, st_ref, tile, heads=all_heads):
        k0 = pl.multiple_of(t * T, T)
        for hh in heads:
            st = st_ref[hh]
            if tile is not None:
                bias = bias_ref[hh, tile]
                st = st + jnp.concatenate([bias, bias], axis=1)
            m_prev = m_scr[hh]
            m_new = jnp.maximum(m_prev, jnp.max(st, axis=0, keepdims=True))
            alpha = jnp.exp2(m_prev - m_new)
            p = jnp.exp2(st - m_new).astype(BF16)
            vb = v_ref[pl.ds(k0, T), hh * LANES:(hh + 1) * LANES]
            pv = jnp.dot(jnp.concatenate([vb.T, ones], axis=0), p, preferred_element_type=F32)
            l_scr[hh] = alpha * l_scr[hh] + pv[2 * HEAD_DIM_A:2 * HEAD_DIM_A + 1]
            acc_scr[hh] = alpha * acc_scr[hh] + pv[:2 * HEAD_DIM_A]
            m_scr[hh] = m_new

    def tick(t, cur, nxt, tile):
        scores(t + 1, nxt, all_heads[:ATT_SKEW])
        for hh in all_heads:
            scores(t + 1, nxt, all_heads[hh + ATT_SKEW:hh + ATT_SKEW + 1])
            softmax_pv(t, cur, tile, (hh,))

    n_far = jnp.maximum(qi - 1, 0)

    def far_pair(i, carry):
        tick(2 * i, sta_scr, stb_scr, None)
        tick(2 * i + 1, stb_scr, sta_scr, None)
        return carry

    lax.fori_loop(0, n_far // 2, far_pair, 0)
    odd = lax.rem(n_far, 2) == 1

    @pl.when(odd)
    def _():
        tick(n_far - 1, sta_scr, stb_scr, None)

    @pl.when(jnp.logical_and(qi >= 1, jnp.logical_not(odd)))
    def _():
        tick(qi - 1, sta_scr, stb_scr, 1)
        softmax_pv(qi, stb_scr, 0)

    @pl.when(jnp.logical_and(qi >= 1, odd))
    def _():
        tick(qi - 1, stb_scr, sta_scr, 1)
        softmax_pv(qi, sta_scr, 0)

    @pl.when(qi == 0)
    def _():
        softmax_pv(0, sta_scr, 0)

    lv = lamv_ref[...]
    lam = (jnp.exp(jnp.sum(lv[0:1] * lv[1:2], axis=-1, keepdims=True))
           - jnp.exp(jnp.sum(lv[2:3] * lv[3:4], axis=-1, keepdims=True)) + lam_init)
    for hh in range(ATT_NH):
        ot = acc_scr[hh] / l_scr[hh]
        ot = ot[:, :T] - lam * ot[:, T:]
        ms = jnp.mean(ot * ot, axis=0, keepdims=True)
        on = (ot * lax.rsqrt(ms + EPS)).T * gs_ref[...] * (1.0 - lam_init)
        z = z_ref[:, hh * LANES:(hh + 1) * LANES].astype(F32)
        o_ref[:, hh * LANES:(hh + 1) * LANES] = (on * _silu(z)).astype(o_ref.dtype)

    start_block(qn_ref)


def _diff_attention(proj, bias_tiles, lamv, g_subln, B, S, lam_init):
    T = ATT_T
    nq = S // T
    H = N_HEADS_A
    d_a = H * 2 * HEAD_DIM_A
    W = ATT_NH * LANES
    G = H // ATT_NH
    return pl.pallas_call(
        functools.partial(_attn_kernel, lam_init=lam_init),
        grid=(B, G, nq),
        in_specs=[
            pl.BlockSpec((4, HEAD_DIM_A), lambda b, g, i: (0, 0)),
            pl.BlockSpec((1, LANES), lambda b, g, i: (0, 0)),
            pl.BlockSpec((T, W), lambda b, g, i: (b * nq + i, g)),
            pl.BlockSpec((T, W), lambda b, g, i: (b * nq + jnp.minimum(i + 1, nq - 1), g)),
            pl.BlockSpec((S, W), lambda b, g, i: (b, G + g)),
            pl.BlockSpec((S, W), lambda b, g, i: (b, 2 * G + g)),
            pl.BlockSpec((T, W), lambda b, g, i: (b * nq + i, 3 * G + g)),
            pl.BlockSpec((ATT_NH, 2, T, T), lambda b, g, i: (g, 0, 0, 0)),
        ],
        out_specs=pl.BlockSpec((T, W), lambda b, g, i: (b * nq + i, g)),
        out_shape=jax.ShapeDtypeStruct((B * S, d_a), BF16),
        scratch_shapes=[pltpu.VMEM((ATT_NH, LANES, 2 * T), BF16),
                        pltpu.VMEM((ATT_NH, T, 2 * T), F32),
                        pltpu.VMEM((ATT_NH, T, 2 * T), F32),
                        pltpu.VMEM((ATT_NH, 2 * HEAD_DIM_A, 2 * T), F32),
                        pltpu.VMEM((ATT_NH, 1, 2 * T), F32),
                        pltpu.VMEM((ATT_NH, 1, 2 * T), F32)],
        compiler_params=_cparams(("parallel", "parallel", "arbitrary")),
        name="diff_attn",
    )(lamv, g_subln, proj, proj, proj, proj, proj, bias_tiles)


def _conv_kernel(u_ref, halo_ref, z_ref, cw_ref, cb_ref, lng_ref, lnb_ref, o_ref, c_scr, y_scr):
    C = o_ref.shape[1]
    TS = o_ref.shape[0]

    def glu(u):
        return u[:, :C].astype(F32) * _sigmoid(u[:, C:].astype(F32))

    halo = glu(halo_ref[...])
    c_scr[0, 0:CONV_HALO] = jnp.where(pl.program_id(1) == 0, jnp.zeros_like(halo), halo)
    c_scr[0, CONV_HALO:CONV_HALO + TS] = glu(u_ref[...])
    c_scr[0, CONV_HALO + TS:] = jnp.zeros((SUBLANES, C), F32)

    def shift(r, carry):
        r0 = pl.multiple_of(r * CONV_RS, CONV_RS)
        win = c_scr[0, pl.ds(r0, CONV_RS + SUBLANES), :]
        for s in range(1, SUBLANES):
            c_scr[s, pl.ds(r0, CONV_RS), :] = win[s:s + CONV_RS]
        return carry

    lax.fori_loop(0, (CONV_HALO + TS) // CONV_RS, shift, 0)

    base = CONV_HALO - (CONV_K - 1)

    for lb in range(C // LANES):
        lanes = slice(lb * LANES, (lb + 1) * LANES)
        taps = [cw_ref[j, :, lanes] for j in range(CONV_K)]
        bias = cb_ref[:, lanes]

        def conv_rows(r, carry, lanes=lanes, taps=taps, bias=bias):
            r0 = pl.multiple_of(r * CONV_ROWS, CONV_ROWS)
            for g in range(CONV_ROWS // SUBLANES):
                acc = bias
                for j in range(CONV_K):
                    off = base + j
                    row = r0 + g * SUBLANES + off - off % SUBLANES
                    acc = acc + taps[j] * c_scr[off % SUBLANES, pl.ds(row, SUBLANES), lanes]
                y_scr[pl.ds(r0 + g * SUBLANES, SUBLANES), lanes] = acc
            return carry

        lax.fori_loop(0, TS // CONV_ROWS, conv_rows, 0)

    pack = 2 * SUBLANES

    def norm_rows(r, carry):
        r0 = pl.multiple_of(r * CONV_LN_ROWS, CONV_LN_ROWS)
        for s in range(CONV_LN_ROWS // pack):
            rs = r0 + s * pack
            z = z_ref[pl.ds(rs, pack), :].astype(F32)
            ys = []
            for g in range(pack // SUBLANES):
                acc = y_scr[pl.ds(rs + g * SUBLANES, SUBLANES), :]
                mu = jnp.mean(acc, axis=-1, keepdims=True)
                xc = acc - mu
                var = jnp.mean(xc * xc, axis=-1, keepdims=True)
                y = xc * lax.rsqrt(var + EPS) * lng_ref[...] + lnb_ref[...]
                ys.append(_silu(y) * _silu(z[g * SUBLANES:(g + 1) * SUBLANES]))
            o_ref[pl.ds(rs, pack), :] = jnp.concatenate(ys, axis=0).astype(o_ref.dtype)
        return carry

    lax.fori_loop(0, TS // CONV_LN_ROWS, norm_rows, 0)


def _conv_module(proj, conv_w, conv_b, ln_g, ln_b, B, S, u_off, z_off):
    C = conv_w.shape[-1]
    TS = CONV_TS
    ns = S // TS
    hb = TS // CONV_HALO
    return pl.pallas_call(
        _conv_kernel,
        grid=(B, ns),
        in_specs=[
            pl.BlockSpec((TS, 2 * C), lambda b, s: (b * ns + s, u_off // (2 * C))),
            pl.BlockSpec((CONV_HALO, 2 * C),
                         lambda b, s: (jnp.maximum((b * ns + s) * hb - 1, 0), u_off // (2 * C))),
            pl.BlockSpec((TS, C), lambda b, s: (b * ns + s, z_off // C)),
            pl.BlockSpec((CONV_K, SUBLANES, C), lambda b, s: (0, 0, 0)),
            pl.BlockSpec((SUBLANES, C), lambda b, s: (0, 0)),
            pl.BlockSpec((SUBLANES, C), lambda b, s: (0, 0)),
            pl.BlockSpec((SUBLANES, C), lambda b, s: (0, 0)),
        ],
        out_specs=pl.BlockSpec((TS, C), lambda b, s: (b * ns + s, 0)),
        out_shape=jax.ShapeDtypeStruct((B * S, C), BF16),
        scratch_shapes=[pltpu.VMEM((SUBLANES, CONV_HALO + TS + SUBLANES, C), F32),
                        pltpu.VMEM((TS, C), F32)],
        compiler_params=_cparams(("parallel", "parallel")),
        name="conv_module",
    )(proj, proj, proj, conv_w, conv_b, ln_g, ln_b)


def _memkv_kernel(mem_ref, g_ref, w_ref, o_ref):
    xf = mem_ref[...]
    ms = jnp.mean(xf * xf, axis=-1, keepdims=True)
    mn = (xf * lax.rsqrt(ms + EPS) * g_ref[...]).astype(BF16)
    o_ref[...] = jnp.dot(mn, w_ref[...], preferred_element_type=F32).astype(o_ref.dtype)


def _mem_kv(mem2, g_mem, w_mem_kv, n_mem):
    TM, D = mem2.shape
    N = w_mem_kv.shape[1]
    return pl.pallas_call(
        _memkv_kernel,
        grid=(TM // n_mem,),
        in_specs=[
            pl.BlockSpec((n_mem, D), lambda b: (b, 0)),
            pl.BlockSpec((1, D), lambda b: (0, 0)),
            pl.BlockSpec((D, N), lambda b: (0, 0)),
        ],
        out_specs=pl.BlockSpec((n_mem, N), lambda b: (b, 0)),
        out_shape=jax.ShapeDtypeStruct((TM, N), BF16),
        compiler_params=_cparams(("parallel",)),
        name="mem_kv",
    )(mem2, g_mem, w_mem_kv)


def _xattn_kernel(q_ref, z_ref, kv_ref, o_ref):
    dx = HEAD_DIM_X
    d_x = N_HEADS_X * dx
    for h in range(N_HEADS_X):
        qh = q_ref[:, h * dx:(h + 1) * dx]
        kh = kv_ref[:, h * dx:(h + 1) * dx]
        vh = kv_ref[:, d_x + h * dx:d_x + (h + 1) * dx]
        s = lax.dot_general(qh, kh, (((1,), (1,)), ((), ())), preferred_element_type=F32)
        m = jnp.max(s, axis=-1, keepdims=True)
        p = jnp.exp2(s - m)
        l = jnp.sum(p, axis=-1, keepdims=True)
        o = jnp.dot(p.astype(BF16), vh, preferred_element_type=F32) / l
        z = z_ref[:, h * dx:(h + 1) * dx].astype(F32)
        o_ref[:, h * dx:(h + 1) * dx] = (o * _silu(z)).astype(o_ref.dtype)


def _cross_attention(proj, kv, B, S, n_mem, q_off, z_off):
    d_x = N_HEADS_X * HEAD_DIM_X
    TQ = XATT_TQ
    nq = S // TQ
    return pl.pallas_call(
        _xattn_kernel,
        grid=(B, nq),
        in_specs=[
            pl.BlockSpec((TQ, d_x), lambda b, i: (b * nq + i, q_off // d_x)),
            pl.BlockSpec((TQ, d_x), lambda b, i: (b * nq + i, z_off // d_x)),
            pl.BlockSpec((n_mem, 2 * d_x), lambda b, i: (b, 0)),
        ],
        out_specs=pl.BlockSpec((TQ, d_x), lambda b, i: (b * nq + i, 0)),
        out_shape=jax.ShapeDtypeStruct((B * S, d_x), BF16),
        compiler_params=_cparams(("parallel", "parallel")),
        name="mem_xattn",
    )(proj, proj, kv)


def _merge_kernel(a_ref, b_ref, c_ref, gt_ref, x_ref, woa_ref, wob_ref, woc_ref, wout_ref,
                  bm_ref, gp_ref, o_ref):
    D = x_ref.shape[1]
    merged = jnp.zeros(x_ref.shape, F32)
    for n, (br_ref, w_ref) in enumerate(((a_ref, woa_ref), (b_ref, wob_ref), (c_ref, woc_ref))):
        yb = jnp.dot(br_ref[...], w_ref[...], preferred_element_type=F32)
        gate = _sigmoid(gt_ref[:, n * D:(n + 1) * D].astype(F32) + bm_ref[:, n * D:(n + 1) * D])
        merged = merged + gate * yb
    y = jnp.dot(merged.astype(BF16), wout_ref[...], preferred_element_type=F32)
    ms = jnp.mean(y * y, axis=-1, keepdims=True)
    o_ref[...] = x_ref[...] + y * lax.rsqrt(ms + EPS) * gp_ref[...]


def _merge(ya, yb, yc, proj, x2, w_oa, w_ob, w_oc, w_out, b_merge, g_post, g_off):
    T, D = x2.shape
    TM = MERGE_TM
    row = lambda i: (i, 0)
    fixed = lambda i: (0, 0)
    return pl.pallas_call(
        _merge_kernel,
        grid=(T // TM,),
        in_specs=[
            pl.BlockSpec((TM, D), row),
            pl.BlockSpec((TM, D), row),
            pl.BlockSpec((TM, D), row),
            pl.BlockSpec((TM, N_BRANCH * D), lambda i: (i, g_off // (N_BRANCH * D))),
            pl.BlockSpec((TM, D), row),
            pl.BlockSpec((D, D), fixed),
            pl.BlockSpec((D, D), fixed),
            pl.BlockSpec((D, D), fixed),
            pl.BlockSpec((D, D), fixed),
            pl.BlockSpec((1, N_BRANCH * D), fixed),
            pl.BlockSpec((1, D), fixed),
        ],
        out_specs=pl.BlockSpec((TM, D), row),
        out_shape=jax.ShapeDtypeStruct((T, D), F32),
        compiler_params=_cparams(("parallel",)),
        name="merge_out",
    )(ya, yb, yc, proj, x2, w_oa, w_ob, w_oc, w_out, b_merge, g_post)


def kernel(x, mem, rel_bias, g_pre, g_mem, w_in, b_merge, lam_q1, lam_k1, lam_q2, lam_k2,
           g_subln, w_oa, conv_w, conv_b, ln_g, ln_b, w_ob, w_mem_kv, w_oc, w_out, g_post):
    B, S, D = x.shape
    n_mem = mem.shape[1]
    depth = w_in.shape[0]
    d_a = N_HEADS_A * 2 * HEAD_DIM_A
    d_conv = conv_w.shape[2]
    d_x = N_HEADS_X * HEAD_DIM_X
    sizes = (d_a, d_a, d_a, d_a, 2 * d_conv, d_conv, d_x, d_x, N_BRANCH * D)
    offs = [int(o) for o in np.cumsum((0,) + sizes)]
    assert offs[-1] == w_in.shape[2] and d_a == d_conv == d_x == D
    assert MAX_DIST <= ATT_T and ATT_T % CHUNK == 0 and CONV_HALO >= CONV_K - 1
    assert (B * S) % PROJ_TM == 0 and offs[-1] % PROJ_TN == 0 and (B * S) % MERGE_TM == 0
    assert S % ATT_T == 0 and N_HEADS_A % ATT_NH == 0 and S % CONV_TS == 0 and S % XATT_TQ == 0
    assert CONV_TS % CONV_LN_ROWS == 0 and CONV_TS % CONV_ROWS == 0 and CONV_TS % CONV_HALO == 0
    assert (CONV_HALO + CONV_TS) % CONV_RS == 0

    x2 = x.reshape(B * S, D)
    mem2 = mem.reshape(B * n_mem, D)
    bias_tiles = _bias_tiles(rel_bias)
    col_scale = (jnp.ones((SUBLANES, offs[-1]), F32).at[:, :d_a].set(Q_SCALE_LOG2)
                 .at[:, offs[6]:offs[7]].set(XQ_SCALE_LOG2))
    for l in range(depth):
        lam_init = 0.8 - 0.6 * math.exp(-0.3 * l)
        proj = _proj(x2, g_pre[l][None], w_in[l].astype(BF16), col_scale)

        lamv = jnp.stack([lam_q1[l], lam_k1[l], lam_q2[l], lam_k2[l]])
        ya = _diff_attention(proj, bias_tiles, lamv, g_subln[l][None], B, S, lam_init)

        rep = lambda v: jnp.broadcast_to(v[..., None, :], v.shape[:-1] + (SUBLANES, v.shape[-1]))
        yb = _conv_module(proj, rep(conv_w[l]), rep(conv_b[l]), rep(ln_g[l]), rep(ln_b[l]),
                          B, S, offs[4], offs[5])

        kv = _mem_kv(mem2, g_mem[l][None], w_mem_kv[l].astype(BF16), n_mem)
        yc = _cross_attention(proj, kv, B, S, n_mem, offs[6], offs[7])

        x2 = _merge(ya, yb, yc, proj, x2, w_oa[l].astype(BF16), w_ob[l].astype(BF16),
                    w_oc[l].astype(BF16), w_out[l].astype(BF16), b_merge[l][None], g_post[l][None],
                    offs[8])
    return x2.reshape(B, S, D)
```

```python
import functools
import math

import jax
import jax.numpy as jnp
import numpy as np
from jax import lax
from jax.experimental import pallas as pl
from jax.experimental.pallas import tpu as pltpu

CHUNK = 64
N_HEADS_A = 8
HEAD_DIM_A = 64
N_HEADS_X = 4
HEAD_DIM_X = 256
CONV_K = 31
N_BRANCH = 3
N_BUCKETS = 32
MAX_DIST = 128
EPS = 1e-6
NEG_INF = -1e30
LOG2_E = math.log2(math.e)
Q_SCALE_LOG2 = HEAD_DIM_A ** -0.5 * LOG2_E
XQ_SCALE_LOG2 = HEAD_DIM_X ** -0.5 * LOG2_E

LANES = 128
SUBLANES = 8
VMEM_LIMIT_BYTES = 48 * 1024 * 1024

PROJ_TM = 1024
PROJ_TN = 4096
ATT_T = 256
ATT_NH = 8
ATT_SKEW = 1
CONV_TS = 512
CONV_HALO = 32
CONV_RS = 32
CONV_LN_ROWS = 128
CONV_ROWS = 128
XATT_TQ = 512
MERGE_TM = 512

BF16 = jnp.bfloat16
F32 = jnp.float32


def _cparams(sem):
    return pltpu.CompilerParams(dimension_semantics=sem, vmem_limit_bytes=VMEM_LIMIT_BYTES)


def _sigmoid(z):
    return 0.5 * jnp.tanh(0.5 * z) + 0.5


def _silu(z):
    return z * _sigmoid(z)


def _proj_kernel(x_ref, g_ref, w_ref, cs_ref, o_ref, h_scr):
    @pl.when(pl.program_id(1) == 0)
    def _():
        xf = x_ref[...]
        ms = jnp.mean(xf * xf, axis=-1, keepdims=True)
        h_scr[...] = (xf * lax.rsqrt(ms + EPS) * g_ref[...]).astype(BF16)

    acc = jnp.dot(h_scr[...], w_ref[...], preferred_element_type=F32)
    tm, tn = acc.shape
    acc = acc.reshape(tm // SUBLANES, SUBLANES, tn) * cs_ref[...][None]
    o_ref[...] = acc.reshape(tm, tn).astype(o_ref.dtype)


def _proj(x2, g_pre, w_in, col_scale):
    T, D = x2.shape
    N = w_in.shape[1]
    return pl.pallas_call(
        _proj_kernel,
        grid=(T // PROJ_TM, N // PROJ_TN),
        in_specs=[
            pl.BlockSpec((PROJ_TM, D), lambda i, j: (i, 0)),
            pl.BlockSpec((1, D), lambda i, j: (0, 0)),
            pl.BlockSpec((D, PROJ_TN), lambda i, j: (0, j)),
            pl.BlockSpec((SUBLANES, PROJ_TN), lambda i, j: (0, j)),
        ],
        out_specs=pl.BlockSpec((PROJ_TM, PROJ_TN), lambda i, j: (i, j)),
        out_shape=jax.ShapeDtypeStruct((T, N), BF16),
        scratch_shapes=[pltpu.VMEM((PROJ_TM, D), BF16)],
        compiler_params=_cparams(("parallel", "arbitrary")),
        name="proj",
    )(x2, g_pre, w_in, col_scale)


def _t5_bucket(rel):
    nb = N_BUCKETS // 2
    max_exact = nb // 2
    ret = (rel > 0).astype(jnp.int32) * nb
    n = jnp.abs(rel)
    nf = jnp.maximum(n, 1).astype(jnp.float32)
    large = max_exact + (jnp.log(nf / max_exact) / math.log(MAX_DIST / max_exact)
                         * (nb - max_exact)).astype(jnp.int32)
    large = jnp.minimum(large, nb - 1)
    return ret + jnp.where(n < max_exact, n, large)


def _bucket_tiles():
    kpos = jnp.arange(ATT_T, dtype=jnp.int32)[:, None]
    qpos = jnp.arange(ATT_T, dtype=jnp.int32)[None, :]
    tiles = []
    for d in (0, -1, -2):
        rel = kpos + d * ATT_T - qpos
        tiles.append(_t5_bucket(rel))
    allowed = (kpos // CHUNK) <= (qpos // CHUNK)
    tiles[0] = jnp.where(allowed, tiles[0], -1)
    return jnp.stack(tiles)


def _bias_kernel(relb_ref, bkt_ref, o_ref):
    h = pl.program_id(0)
    bkt = bkt_ref[...]
    res = jnp.zeros(bkt.shape, F32)
    for b in range(N_BUCKETS):
        res = jnp.where(bkt == b, relb_ref[b, h], res)
    near = res[0:2] - res[2, 0:1, 0:1]
    o_ref[0] = jnp.where(bkt[0:2] < 0, NEG_INF, near * LOG2_E)


def _bias_tiles(rel_bias):
    bkt = _bucket_tiles()
    return pl.pallas_call(
        _bias_kernel,
        grid=(N_HEADS_A,),
        in_specs=[
            pl.BlockSpec(memory_space=pltpu.SMEM),
            pl.BlockSpec((3, ATT_T, ATT_T), lambda h: (0, 0, 0)),
        ],
        out_specs=pl.BlockSpec((1, 2, ATT_T, ATT_T), lambda h: (h, 0, 0, 0)),
        out_shape=jax.ShapeDtypeStruct((N_HEADS_A, 2, ATT_T, ATT_T), F32),
        compiler_params=_cparams(("arbitrary",)),
        name="bias_tiles",
    )(rel_bias, bkt)


def _attn_kernel(lamv_ref, gs_ref, q_ref, qn_ref, k_ref, v_ref, z_ref, bias_ref, o_ref,
                 qq_scr, sta_scr, stb_scr, acc_scr, m_scr, l_scr, *, lam_init):
    T = ATT_T
    qi = pl.program_id(2)
    ones = jnp.ones((2 * SUBLANES, T), BF16)

    all_heads = tuple(range(ATT_NH))

    def scores(t, st_ref, heads=all_heads):
        k0 = pl.multiple_of(t * T, T)
        for hh in heads:
            kb = k_ref[pl.ds(k0, T), hh * LANES:(hh + 1) * LANES]
            st_ref[hh] = jnp.dot(kb, qq_scr[hh], preferred_element_type=F32)

    def start_block(src_ref):
        lane = lax.broadcasted_iota(jnp.int32, (T, LANES), 1)
        for hh in range(ATT_NH):
            q = src_ref[:, hh * LANES:(hh + 1) * LANES]
            zero = jnp.zeros_like(q)
            qq_scr[hh, :, 0:T] = jnp.where(lane < HEAD_DIM_A, q, zero).T
            qq_scr[hh, :, T:2 * T] = jnp.where(lane >= HEAD_DIM_A, q, zero).T
        m_scr[...] = jnp.full(m_scr.shape, -jnp.inf, F32)
        l_scr[...] = jnp.zeros(l_scr.shape, F32)
        acc_scr[...] = jnp.zeros(acc_scr.shape, F32)
        scores(0, sta_scr)

    @pl.when(qi == 0)
    def _():
        start_block(q_ref)

    def softmax_pv(t, st_ref, tile, heads=all_heads):
        k0 = pl.multiple_of(t * T, T)
        for hh in heads:
            st = st_ref[hh]
            if tile is not None:
                bias = bias_ref[hh, tile]
                st = st + jnp.concatenate([bias, bias], axis=1)
            m_prev = m_scr[hh]
            m_new = jnp.maximum(m_prev, jnp.max(st, axis=0, keepdims=True))
            alpha = jnp.exp2(m_prev - m_new)
            p = jnp.exp2(st - m_new).astype(BF16)
            vb = v_ref[pl.ds(k0, T), hh * LANES:(hh + 1) * LANES]
            pv = jnp.dot(jnp.concatenate([vb.T, ones], axis=0), p, preferred_element_type=F32)
            l_scr[hh] = alpha * l_scr[hh] + pv[2 * HEAD_DIM_A:2 * HEAD_DIM_A + 1]
            acc_scr[hh] = alpha * acc_scr[hh] + pv[:2 * HEAD_DIM_A]
            m_scr[hh] = m_new

    def tick(t, cur, nxt, tile):
        scores(t + 1, nxt, all_heads[:ATT_SKEW])
        for hh in all_heads:
            scores(t + 1, nxt, all_heads[hh + ATT_SKEW:hh + ATT_SKEW + 1])
            softmax_pv(t, cur, tile, (hh,))

    n_far = jnp.maximum(qi - 1, 0)

    def far_pair(i, carry):
        tick(2 * i, sta_scr, stb_scr, None)
        tick(2 * i + 1, stb_scr, sta_scr, None)
        return carry

    lax.fori_loop(0, n_far // 2, far_pair, 0)
    odd = lax.rem(n_far, 2) == 1

    @pl.when(odd)
    def _():
        tick(n_far - 1, sta_scr, stb_scr, None)

    @pl.when(jnp.logical_and(qi >= 1, jnp.logical_not(odd)))
    def _():
        tick(qi - 1, sta_scr, stb_scr, 1)
        softmax_pv(qi, stb_scr, 0)

    @pl.when(jnp.logical_and(qi >= 1, odd))
    def _():
        tick(qi - 1, stb_scr, sta_scr, 1)
        softmax_pv(qi, sta_scr, 0)

    @pl.when(qi == 0)
    def _():
        softmax_pv(0, sta_scr, 0)

    lv = lamv_ref[...]
    lam = (jnp.exp(jnp.sum(lv[0:1] * lv[1:2], axis=-1, keepdims=True))
           - jnp.exp(jnp.sum(lv[2:3] * lv[3:4], axis=-1, keepdims=True)) + lam_init)
    for hh in range(ATT_NH):
        ot = acc_scr[hh] / l_scr[hh]
        ot = ot[:, :T] - lam * ot[:, T:]
        ms = jnp.mean(ot * ot, axis=0, keepdims=True)
        on = (ot * lax.rsqrt(ms + EPS)).T * gs_ref[...] * (1.0 - lam_init)
        z = z_ref[:, hh * LANES:(hh + 1) * LANES].astype(F32)
        o_ref[:, hh * LANES:(hh + 1) * LANES] = (on * _silu(z)).astype(o_ref.dtype)

    start_block(qn_ref)


def _diff_attention(proj, bias_tiles, lamv, g_subln, B, S, lam_init):
    T = ATT_T
    nq = S // T
    H = N_HEADS_A
    d_a = H * 2 * HEAD_DIM_A
    W = ATT_NH * LANES
    G = H // ATT_NH
    return pl.pallas_call(
        functools.partial(_attn_kernel, lam_init=lam_init),
        grid=(B, G, nq),
        in_specs=[
            pl.BlockSpec((4, HEAD_DIM_A), lambda b, g, i: (0, 0)),
            pl.BlockSpec((1, LANES), lambda b, g, i: (0, 0)),
            pl.BlockSpec((T, W), lambda b, g, i: (b * nq + i, g)),
            pl.BlockSpec((T, W), lambda b, g, i: (b * nq + jnp.minimum(i + 1, nq - 1), g)),
            pl.BlockSpec((S, W), lambda b, g, i: (b, G + g)),
            pl.BlockSpec((S, W), lambda b, g, i: (b, 2 * G + g)),
            pl.BlockSpec((T, W), lambda b, g, i: (b * nq + i, 3 * G + g)),
            pl.BlockSpec((ATT_NH, 2, T, T), lambda b, g, i: (g, 0, 0, 0)),
        ],
        out_specs=pl.BlockSpec((T, W), lambda b, g, i: (b * nq + i, g)),
        out_shape=jax.ShapeDtypeStruct((B * S, d_a), BF16),
        scratch_shapes=[pltpu.VMEM((ATT_NH, LANES, 2 * T), BF16),
                        pltpu.VMEM((ATT_NH, T, 2 * T), F32),
                        pltpu.VMEM((ATT_NH, T, 2 * T), F32),
                        pltpu.VMEM((ATT_NH, 2 * HEAD_DIM_A, 2 * T), F32),
                        pltpu.VMEM((ATT_NH, 1, 2 * T), F32),
                        pltpu.VMEM((ATT_NH, 1, 2 * T), F32)],
        compiler_params=_cparams(("parallel", "parallel", "arbitrary")),
        name="diff_attn",
    )(lamv, g_subln, proj, proj, proj, proj, proj, bias_tiles)


def _conv_kernel(u_ref, halo_ref, z_ref, cw_ref, cb_ref, lng_ref, lnb_ref, o_ref, c_scr, y_scr):
    C = o_ref.shape[1]
    TS = o_ref.shape[0]

    def glu(u):
        return u[:, :C].astype(F32) * _sigmoid(u[:, C:].astype(F32))

    halo = glu(halo_ref[...])
    c_scr[0, 0:CONV_HALO] = jnp.where(pl.program_id(1) == 0, jnp.zeros_like(halo), halo)
    c_scr[0, CONV_HALO:CONV_HALO + TS] = glu(u_ref[...])
    c_scr[0, CONV_HALO + TS:] = jnp.zeros((SUBLANES, C), F32)

    def shift(r, carry):
        r0 = pl.multiple_of(r * CONV_RS, CONV_RS)
        win = c_scr[0, pl.ds(r0, CONV_RS + SUBLANES), :]
        for s in range(1, SUBLANES):
            c_scr[s, pl.ds(r0, CONV_RS), :] = win[s:s + CONV_RS]
        return carry

    lax.fori_loop(0, (CONV_HALO + TS) // CONV_RS, shift, 0)

    base = CONV_HALO - (CONV_K - 1)

    for lb in range(C // LANES):
        lanes = slice(lb * LANES, (lb + 1) * LANES)
        taps = [cw_ref[j, :, lanes] for j in range(CONV_K)]
        bias = cb_ref[:, lanes]

        def conv_rows(r, carry, lanes=lanes, taps=taps, bias=bias):
            r0 = pl.multiple_of(r * CONV_ROWS, CONV_ROWS)
            for g in range(CONV_ROWS // SUBLANES):
                acc = bias
                for j in range(CONV_K):
                    off = base + j
                    row = r0 + g * SUBLANES + off - off % SUBLANES
                    acc = acc + taps[j] * c_scr[off % SUBLANES, pl.ds(row, SUBLANES), lanes]
                y_scr[pl.ds(r0 + g * SUBLANES, SUBLANES), lanes] = acc
            return carry

        lax.fori_loop(0, TS // CONV_ROWS, conv_rows, 0)

    pack = 2 * SUBLANES

    def norm_rows(r, carry):
        r0 = pl.multiple_of(r * CONV_LN_ROWS, CONV_LN_ROWS)
        for s in range(CONV_LN_ROWS // pack):
            rs = r0 + s * pack
            z = z_ref[pl.ds(rs, pack), :].astype(F32)
            ys = []
            for g in range(pack // SUBLANES):
                acc = y_scr[pl.ds(rs + g * SUBLANES, SUBLANES), :]
                mu = jnp.mean(acc, axis=-1, keepdims=True)
                xc = acc - mu
                var = jnp.mean(xc * xc, axis=-1, keepdims=True)
                y = xc * lax.rsqrt(var + EPS) * lng_ref[...] + lnb_ref[...]
                ys.append(_silu(y) * _silu(z[g * SUBLANES:(g + 1) * SUBLANES]))
            o_ref[pl.ds(rs, pack), :] = jnp.concatenate(ys, axis=0).astype(o_ref.dtype)
        return carry

    lax.fori_loop(0, TS // CONV_LN_ROWS, norm_rows, 0)


def _conv_module(proj, conv_w, conv_b, ln_g, ln_b, B, S, u_off, z_off):
    C = conv_w.shape[-1]
    TS = CONV_TS
    ns = S // TS
    hb = TS // CONV_HALO
    return pl.pallas_call(
        _conv_kernel,
        grid=(B, ns),
        in_specs=[
            pl.BlockSpec((TS, 2 * C), lambda b, s: (b * ns + s, u_off // (2 * C))),
            pl.BlockSpec((CONV_HALO, 2 * C),
                         lambda b, s: (jnp.maximum((b * ns + s) * hb - 1, 0), u_off // (2 * C))),
            pl.BlockSpec((TS, C), lambda b, s: (b * ns + s, z_off // C)),
            pl.BlockSpec((CONV_K, SUBLANES, C), lambda b, s: (0, 0, 0)),
            pl.BlockSpec((SUBLANES, C), lambda b, s: (0, 0)),
            pl.BlockSpec((SUBLANES, C), lambda b, s: (0, 0)),
            pl.BlockSpec((SUBLANES, C), lambda b, s: (0, 0)),
        ],
        out_specs=pl.BlockSpec((TS, C), lambda b, s: (b * ns + s, 0)),
        out_shape=jax.ShapeDtypeStruct((B * S, C), BF16),
        scratch_shapes=[pltpu.VMEM((SUBLANES, CONV_HALO + TS + SUBLANES, C), F32),
                        pltpu.VMEM((TS, C), F32)],
        compiler_params=_cparams(("parallel", "parallel")),
        name="conv_module",
    )(proj, proj, proj, conv_w, conv_b, ln_g, ln_b)


def _memkv_kernel(mem_ref, g_ref, w_ref, o_ref):
    xf = mem_ref[...]
    ms = jnp.mean(xf * xf, axis=-1, keepdims=True)
    mn = (xf * lax.rsqrt(ms + EPS) * g_ref[...]).astype(BF16)
    o_ref[...] = jnp.dot(mn, w_ref[...], preferred_element_type=F32).astype(o_ref.dtype)


def _mem_kv(mem2, g_mem, w_mem_kv, n_mem):
    TM, D = mem2.shape
    N = w_mem_kv.shape[1]
    return pl.pallas_call(
        _memkv_kernel,
        grid=(TM // n_mem,),
        in_specs=[
            pl.BlockSpec((n_mem, D), lambda b: (b, 0)),
            pl.BlockSpec((1, D), lambda b: (0, 0)),
            pl.BlockSpec((D, N), lambda b: (0, 0)),
        ],
        out_specs=pl.BlockSpec((n_mem, N), lambda b: (b, 0)),
        out_shape=jax.ShapeDtypeStruct((TM, N), BF16),
        compiler_params=_cparams(("parallel",)),
        name="mem_kv",
    )(mem2, g_mem, w_mem_kv)


def _xattn_kernel(q_ref, z_ref, kv_ref, o_ref):
    dx = HEAD_DIM_X
    d_x = N_HEADS_X * dx
    for h in range(N_HEADS_X):
        qh = q_ref[:, h * dx:(h + 1) * dx]
        kh = kv_ref[:, h * dx:(h + 1) * dx]
        vh = kv_ref[:, d_x + h * dx:d_x + (h + 1) * dx]
        s = lax.dot_general(qh, kh, (((1,), (1,)), ((), ())), preferred_element_type=F32)
        m = jnp.max(s, axis=-1, keepdims=True)
        p = jnp.exp2(s - m)
        l = jnp.sum(p, axis=-1, keepdims=True)
        o = jnp.dot(p.astype(BF16), vh, preferred_element_type=F32) / l
        z = z_ref[:, h * dx:(h + 1) * dx].astype(F32)
        o_ref[:, h * dx:(h + 1) * dx] = (o * _silu(z)).astype(o_ref.dtype)


def _cross_attention(proj, kv, B, S, n_mem, q_off, z_off):
    d_x = N_HEADS_X * HEAD_DIM_X
    TQ = XATT_TQ
    nq = S // TQ
    return pl.pallas_call(
        _xattn_kernel,
        grid=(B, nq),
        in_specs=[
            pl.BlockSpec((TQ, d_x), lambda b, i: (b * nq + i, q_off // d_x)),
            pl.BlockSpec((TQ, d_x), lambda b, i: (b * nq + i, z_off // d_x)),
            pl.BlockSpec((n_mem, 2 * d_x), lambda b, i: (b, 0)),
        ],
        out_specs=pl.BlockSpec((TQ, d_x), lambda b, i: (b * nq + i, 0)),
        out_shape=jax.ShapeDtypeStruct((B * S, d_x), BF16),
        compiler_params=_cparams(("parallel", "parallel")),
        name="mem_xattn",
    )(proj, proj, kv)


def _merge_kernel(a_ref, b_ref, c_ref, gt_ref, x_ref, woa_ref, wob_ref, woc_ref, wout_ref,
                  bm_ref, gp_ref, o_ref):
    D = x_ref.shape[1]
    merged = jnp.zeros(x_ref.shape, F32)
    for n, (br_ref, w_ref) in enumerate(((a_ref, woa_ref), (b_ref, wob_ref), (c_ref, woc_ref))):
        yb = jnp.dot(br_ref[...], w_ref[...], preferred_element_type=F32)
        gate = _sigmoid(gt_ref[:, n * D:(n + 1) * D].astype(F32) + bm_ref[:, n * D:(n + 1) * D])
        merged = merged + gate * yb
    y = jnp.dot(merged.astype(BF16), wout_ref[...], preferred_element_type=F32)
    ms = jnp.mean(y * y, axis=-1, keepdims=True)
    o_ref[...] = x_ref[...] + y * lax.rsqrt(ms + EPS) * gp_ref[...]


def _merge(ya, yb, yc, proj, x2, w_oa, w_ob, w_oc, w_out, b_merge, g_post, g_off):
    T, D = x2.shape
    TM = MERGE_TM
    row = lambda i: (i, 0)
    fixed = lambda i: (0, 0)
    return pl.pallas_call(
        _merge_kernel,
        grid=(T // TM,),
        in_specs=[
            pl.BlockSpec((TM, D), row),
            pl.BlockSpec((TM, D), row),
            pl.BlockSpec((TM, D), row),
            pl.BlockSpec((TM, N_BRANCH * D), lambda i: (i, g_off // (N_BRANCH * D))),
            pl.BlockSpec((TM, D), row),
            pl.BlockSpec((D, D), fixed),
            pl.BlockSpec((D, D), fixed),
            pl.BlockSpec((D, D), fixed),
            pl.BlockSpec((D, D), fixed),
            pl.BlockSpec((1, N_BRANCH * D), fixed),
            pl.BlockSpec((1, D), fixed),
        ],
        out_specs=pl.BlockSpec((TM, D), row),
        out_shape=jax.ShapeDtypeStruct((T, D), F32),
        compiler_params=_cparams(("parallel",)),
        name="merge_out",
    )(ya, yb, yc, proj, x2, w_oa, w_ob, w_oc, w_out, b_merge, g_post)


def kernel(x, mem, rel_bias, g_pre, g_mem, w_in, b_merge, lam_q1, lam_k1, lam_q2, lam_k2,
           g_subln, w_oa, conv_w, conv_b, ln_g, ln_b, w_ob, w_mem_kv, w_oc, w_out, g_post):
    B, S, D = x.shape
    n_mem = mem.shape[1]
    depth = w_in.shape[0]
    d_a = N_HEADS_A * 2 * HEAD_DIM_A
    d_conv = conv_w.shape[2]
    d_x = N_HEADS_X * HEAD_DIM_X
    sizes = (d_a, d_a, d_a, d_a, 2 * d_conv, d_conv, d_x, d_x, N_BRANCH * D)
    offs = [int(o) for o in np.cumsum((0,) + sizes)]
    assert offs[-1] == w_in.shape[2] and d_a == d_conv == d_x == D
    assert MAX_DIST <= ATT_T and ATT_T % CHUNK == 0 and CONV_HALO >= CONV_K - 1
    assert (B * S) % PROJ_TM == 0 and offs[-1] % PROJ_TN == 0 and (B * S) % MERGE_TM == 0
    assert S % ATT_T == 0 and N_HEADS_A % ATT_NH == 0 and S % CONV_TS == 0 and S % XATT_TQ == 0
    assert CONV_TS % CONV_LN_ROWS == 0 and CONV_TS % CONV_ROWS == 0 and CONV_TS % CONV_HALO == 0
    assert (CONV_HALO + CONV_TS) % CONV_RS == 0

    x2 = x.reshape(B * S, D)
    mem2 = mem.reshape(B * n_mem, D)
    bias_tiles = _bias_tiles(rel_bias)
    col_scale = (jnp.ones((SUBLANES, offs[-1]), F32).at[:, :d_a].set(Q_SCALE_LOG2)
                 .at[:, offs[6]:offs[7]].set(XQ_SCALE_LOG2))
    for l in range(depth):
        lam_init = 0.8 - 0.6 * math.exp(-0.3 * l)
        proj = _proj(x2, g_pre[l][None], w_in[l].astype(BF16), col_scale)

        lamv = jnp.stack([lam_q1[l], lam_k1[l], lam_q2[l], lam_k2[l]])
        ya = _diff_attention(proj, bias_tiles, lamv, g_subln[l][None], B, S, lam_init)

        rep = lambda v: jnp.broadcast_to(v[..., None, :], v.shape[:-1] + (SUBLANES, v.shape[-1]))
        yb = _conv_module(proj, rep(conv_w[l]), rep(conv_b[l]), rep(ln_g[l]), rep(ln_b[l]),
                          B, S, offs[4], offs[5])

        kv = _mem_kv(mem2, g_mem[l][None], w_mem_kv[l].astype(BF16), n_mem)
        yc = _cross_attention(proj, kv, B, S, n_mem, offs[6], offs[7])

        x2 = _merge(ya, yb, yc, proj, x2, w_oa[l].astype(BF16), w_ob[l].astype(BF16),
                    w_oc[l].astype(BF16), w_out[l].astype(BF16), b_merge[l][None], g_post[l][None],
                    offs[8])
    return x2.reshape(B, S, D)
```

```python
import functools
import math

import jax
import jax.numpy as jnp
import numpy as np
from jax import lax
from jax.experimental import pallas as pl
from jax.experimental.pallas import tpu as pltpu

CHUNK = 64
N_HEADS_A = 8
HEAD_DIM_A = 64
N_HEADS_X = 4
HEAD_DIM_X = 256
CONV_K = 31
N_BRANCH = 3
N_BUCKETS = 32
MAX_DIST = 128
EPS = 1e-6
NEG_INF = -1e30
LOG2_E = math.log2(math.e)
Q_SCALE_LOG2 = HEAD_DIM_A ** -0.5 * LOG2_E
XQ_SCALE_LOG2 = HEAD_DIM_X ** -0.5 * LOG2_E

LANES = 128
SUBLANES = 8
VMEM_LIMIT_BYTES = 48 * 1024 * 1024

PROJ_TM = 1024
PROJ_TN = 4096
ATT_T = 256
ATT_NH = 8
ATT_SKEW = 1
CONV_TS = 512
CONV_HALO = 32
CONV_RS = 32
CONV_LN_ROWS = 128
CONV_ROWS = 256
XATT_TQ = 512
MERGE_TM = 512

BF16 = jnp.bfloat16
F32 = jnp.float32


def _cparams(sem):
    return pltpu.CompilerParams(dimension_semantics=sem, vmem_limit_bytes=VMEM_LIMIT_BYTES)


def _sigmoid(z):
    return 0.5 * jnp.tanh(0.5 * z) + 0.5


def _silu(z):
    return z * _sigmoid(z)


def _proj_kernel(x_ref, g_ref, w_ref, cs_ref, o_ref, h_scr):
    @pl.when(pl.program_id(1) == 0)
    def _():
        xf = x_ref[...]
        ms = jnp.mean(xf * xf, axis=-1, keepdims=True)
        h_scr[...] = (xf * lax.rsqrt(ms + EPS) * g_ref[...]).astype(BF16)

    acc = jnp.dot(h_scr[...], w_ref[...], preferred_element_type=F32)
    tm, tn = acc.shape
    acc = acc.reshape(tm // SUBLANES, SUBLANES, tn) * cs_ref[...][None]
    o_ref[...] = acc.reshape(tm, tn).astype(o_ref.dtype)


def _proj(x2, g_pre, w_in, col_scale):
    T, D = x2.shape
    N = w_in.shape[1]
    return pl.pallas_call(
        _proj_kernel,
        grid=(T // PROJ_TM, N // PROJ_TN),
        in_specs=[
            pl.BlockSpec((PROJ_TM, D), lambda i, j: (i, 0)),
            pl.BlockSpec((1, D), lambda i, j: (0, 0)),
            pl.BlockSpec((D, PROJ_TN), lambda i, j: (0, j)),
            pl.BlockSpec((SUBLANES, PROJ_TN), lambda i, j: (0, j)),
        ],
        out_specs=pl.BlockSpec((PROJ_TM, PROJ_TN), lambda i, j: (i, j)),
        out_shape=jax.ShapeDtypeStruct((T, N), BF16),
        scratch_shapes=[pltpu.VMEM((PROJ_TM, D), BF16)],
        compiler_params=_cparams(("parallel", "arbitrary")),
        name="proj",
    )(x2, g_pre, w_in, col_scale)


def _t5_bucket(rel):
    nb = N_BUCKETS // 2
    max_exact = nb // 2
    ret = (rel > 0).astype(jnp.int32) * nb
    n = jnp.abs(rel)
    nf = jnp.maximum(n, 1).astype(jnp.float32)
    large = max_exact + (jnp.log(nf / max_exact) / math.log(MAX_DIST / max_exact)
                         * (nb - max_exact)).astype(jnp.int32)
    large = jnp.minimum(large, nb - 1)
    return ret + jnp.where(n < max_exact, n, large)


def _bucket_tiles():
    kpos = jnp.arange(ATT_T, dtype=jnp.int32)[:, None]
    qpos = jnp.arange(ATT_T, dtype=jnp.int32)[None, :]
    tiles = []
    for d in (0, -1, -2):
        rel = kpos + d * ATT_T - qpos
        tiles.append(_t5_bucket(rel))
    allowed = (kpos // CHUNK) <= (qpos // CHUNK)
    tiles[0] = jnp.where(allowed, tiles[0], -1)
    return jnp.stack(tiles)


def _bias_kernel(relb_ref, bkt_ref, o_ref):
    h = pl.program_id(0)
    bkt = bkt_ref[...]
    res = jnp.zeros(bkt.shape, F32)
    for b in range(N_BUCKETS):
        res = jnp.where(bkt == b, relb_ref[b, h], res)
    near = res[0:2] - res[2, 0:1, 0:1]
    o_ref[0] = jnp.where(bkt[0:2] < 0, NEG_INF, near * LOG2_E)


def _bias_tiles(rel_bias):
    bkt = _bucket_tiles()
    return pl.pallas_call(
        _bias_kernel,
        grid=(N_HEADS_A,),
        in_specs=[
            pl.BlockSpec(memory_space=pltpu.SMEM),
            pl.BlockSpec((3, ATT_T, ATT_T), lambda h: (0, 0, 0)),
        ],
        out_specs=pl.BlockSpec((1, 2, ATT_T, ATT_T), lambda h: (h, 0, 0, 0)),
        out_shape=jax.ShapeDtypeStruct((N_HEADS_A, 2, ATT_T, ATT_T), F32),
        compiler_params=_cparams(("arbitrary",)),
        name="bias_tiles",
    )(rel_bias, bkt)


def _attn_kernel(lamv_ref, gs_ref, q_ref, qn_ref, k_ref, v_ref, z_ref, bias_ref, o_ref,
                 qq_scr, sta_scr, stb_scr, acc_scr, m_scr, l_scr, *, lam_init):
    T = ATT_T
    qi = pl.program_id(2)
    ones = jnp.ones((2 * SUBLANES, T), BF16)

    all_heads = tuple(range(ATT_NH))

    def scores(t, st_ref, heads=all_heads):
        k0 = pl.multiple_of(t * T, T)
        for hh in heads:
            kb = k_ref[pl.ds(k0, T), hh * LANES:(hh + 1) * LANES]
            st_ref[hh] = jnp.dot(kb, qq_scr[hh], preferred_element_type=F32)

    def start_block(src_ref):
        lane = lax.broadcasted_iota(jnp.int32, (T, LANES), 1)
        for hh in range(ATT_NH):
            q = src_ref[:, hh * LANES:(hh + 1) * LANES]
            zero = jnp.zeros_like(q)
            qq_scr[hh, :, 0:T] = jnp.where(lane < HEAD_DIM_A, q, zero).T
            qq_scr[hh, :, T:2 * T] = jnp.where(lane >= HEAD_DIM_A, q, zero).T
        m_scr[...] = jnp.full(m_scr.shape, -jnp.inf, F32)
        l_scr[...] = jnp.zeros(l_scr.shape, F32)
        acc_scr[...] = jnp.zeros(acc_scr.shape, F32)
        scores(0, sta_scr)

    @pl.when(qi == 0)
    def _():
        start_block(q_ref)

    def softmax_pv(t, st_ref, tile, heads=all_heads):
        k0 = pl.multiple_of(t * T, T)
        for hh in heads:
            st = st_ref[hh]
            if tile is not None:
                bias = bias_ref[hh, tile]
                st = st + jnp.concatenate([bias, bias], axis=1)
            m_prev = m_scr[hh]
            m_new = jnp.maximum(m_prev, jnp.max(st, axis=0, keepdims=True))
            alpha = jnp.exp2(m_prev - m_new)
            p = jnp.exp2(st - m_new).astype(BF16)
            vb = v_ref[pl.ds(k0, T), hh * LANES:(hh + 1) * LANES]
            pv = jnp.dot(jnp.concatenate([vb.T, ones], axis=0), p, preferred_element_type=F32)
            l_scr[hh] = alpha * l_scr[hh] + pv[2 * HEAD_DIM_A:2 * HEAD_DIM_A + 1]
            acc_scr[hh] = alpha * acc_scr[hh] + pv[:2 * HEAD_DIM_A]
            m_scr[hh] = m_new

    def tick(t, cur, nxt, tile):
        scores(t + 1, nxt, all_heads[:ATT_SKEW])
        for hh in all_heads:
            scores(t + 1, nxt, all_heads[hh + ATT_SKEW:hh + ATT_SKEW + 1])
            softmax_pv(t, cur, tile, (hh,))

    n_far = jnp.maximum(qi - 1, 0)

    def far_pair(i, carry):
        tick(2 * i, sta_scr, stb_scr, None)
        tick(2 * i + 1, stb_scr, sta_scr, None)
        return carry

    lax.fori_loop(0, n_far // 2, far_pair, 0)
    odd = lax.rem(n_far, 2) == 1

    @pl.when(odd)
    def _():
        tick(n_far - 1, sta_scr, stb_scr, None)

    @pl.when(jnp.logical_and(qi >= 1, jnp.logical_not(odd)))
    def _():
        tick(qi - 1, sta_scr, stb_scr, 1)
        softmax_pv(qi, stb_scr, 0)

    @pl.when(jnp.logical_and(qi >= 1, odd))
    def _():
        tick(qi - 1, stb_scr, sta_scr, 1)
        softmax_pv(qi, sta_scr, 0)

    @pl.when(qi == 0)
    def _():
        softmax_pv(0, sta_scr, 0)

    lv = lamv_ref[...]
    lam = (jnp.exp(jnp.sum(lv[0:1] * lv[1:2], axis=-1, keepdims=True))
           - jnp.exp(jnp.sum(lv[2:3] * lv[3:4], axis=-1, keepdims=True)) + lam_init)
    for hh in range(ATT_NH):
        ot = acc_scr[hh] / l_scr[hh]
        ot = ot[:, :T] - lam * ot[:, T:]
        ms = jnp.mean(ot * ot, axis=0, keepdims=True)
        on = (ot * lax.rsqrt(ms + EPS)).T * gs_ref[...] * (1.0 - lam_init)
        z = z_ref[:, hh * LANES:(hh + 1) * LANES].astype(F32)
        o_ref[:, hh * LANES:(hh + 1) * LANES] = (on * _silu(z)).astype(o_ref.dtype)

    start_block(qn_ref)


def _diff_attention(proj, bias_tiles, lamv, g_subln, B, S, lam_init):
    T = ATT_T
    nq = S // T
    H = N_HEADS_A
    d_a = H * 2 * HEAD_DIM_A
    W = ATT_NH * LANES
    G = H // ATT_NH
    return pl.pallas_call(
        functools.partial(_attn_kernel, lam_init=lam_init),
        grid=(B, G, nq),
        in_specs=[
            pl.BlockSpec((4, HEAD_DIM_A), lambda b, g, i: (0, 0)),
            pl.BlockSpec((1, LANES), lambda b, g, i: (0, 0)),
            pl.BlockSpec((T, W), lambda b, g, i: (b * nq + i, g)),
            pl.BlockSpec((T, W), lambda b, g, i: (b * nq + jnp.minimum(i + 1, nq - 1), g)),
            pl.BlockSpec((S, W), lambda b, g, i: (b, G + g)),
            pl.BlockSpec((S, W), lambda b, g, i: (b, 2 * G + g)),
            pl.BlockSpec((T, W), lambda b, g, i: (b * nq + i, 3 * G + g)),
            pl.BlockSpec((ATT_NH, 2, T, T), lambda b, g, i: (g, 0, 0, 0)),
        ],
        out_specs=pl.BlockSpec((T, W), lambda b, g, i: (b * nq + i, g)),
        out_shape=jax.ShapeDtypeStruct((B * S, d_a), BF16),
        scratch_shapes=[pltpu.VMEM((ATT_NH, LANES, 2 * T), BF16),
                        pltpu.VMEM((ATT_NH, T, 2 * T), F32),
                        pltpu.VMEM((ATT_NH, T, 2 * T), F32),
                        pltpu.VMEM((ATT_NH, 2 * HEAD_DIM_A, 2 * T), F32),
                        pltpu.VMEM((ATT_NH, 1, 2 * T), F32),
                        pltpu.VMEM((ATT_NH, 1, 2 * T), F32)],
        compiler_params=_cparams(("parallel", "parallel", "arbitrary")),
        name="diff_attn",
    )(lamv, g_subln, proj, proj, proj, proj, proj, bias_tiles)


def _conv_kernel(u_ref, halo_ref, z_ref, cw_ref, cb_ref, lng_ref, lnb_ref, o_ref, c_scr, y_scr):
    C = o_ref.shape[1]
    TS = o_ref.shape[0]

    def glu(u):
        return u[:, :C].astype(F32) * _sigmoid(u[:, C:].astype(F32))

    halo = glu(halo_ref[...])
    c_scr[0, 0:CONV_HALO] = jnp.where(pl.program_id(1) == 0, jnp.zeros_like(halo), halo)
    c_scr[0, CONV_HALO:CONV_HALO + TS] = glu(u_ref[...])
    c_scr[0, CONV_HALO + TS:] = jnp.zeros((SUBLANES, C), F32)

    def shift(r, carry):
        r0 = pl.multiple_of(r * CONV_RS, CONV_RS)
        win = c_scr[0, pl.ds(r0, CONV_RS + SUBLANES), :]
        for s in range(1, SUBLANES):
            c_scr[s, pl.ds(r0, CONV_RS), :] = win[s:s + CONV_RS]
        return carry

    lax.fori_loop(0, (CONV_HALO + TS) // CONV_RS, shift, 0)

    base = CONV_HALO - (CONV_K - 1)

    for lb in range(C // LANES):
        lanes = slice(lb * LANES, (lb + 1) * LANES)
        taps = [cw_ref[j, :, lanes] for j in range(CONV_K)]
        bias = cb_ref[:, lanes]

        def conv_rows(r, carry, lanes=lanes, taps=taps, bias=bias):
            r0 = pl.multiple_of(r * CONV_ROWS, CONV_ROWS)
            for g in range(CONV_ROWS // SUBLANES):
                acc = bias
                for j in range(CONV_K):
                    off = base + j
                    row = r0 + g * SUBLANES + off - off % SUBLANES
                    acc = acc + taps[j] * c_scr[off % SUBLANES, pl.ds(row, SUBLANES), lanes]
                y_scr[pl.ds(r0 + g * SUBLANES, SUBLANES), lanes] = acc
            return carry

        lax.fori_loop(0, TS // CONV_ROWS, conv_rows, 0)

    pack = 2 * SUBLANES

    def norm_rows(r, carry):
        r0 = pl.multiple_of(r * CONV_LN_ROWS, CONV_LN_ROWS)
        for s in range(CONV_LN_ROWS // pack):
            rs = r0 + s * pack
            z = z_ref[pl.ds(rs, pack), :].astype(F32)
            ys = []
            for g in range(pack // SUBLANES):
                acc = y_scr[pl.ds(rs + g * SUBLANES, SUBLANES), :]
                mu = jnp.mean(acc, axis=-1, keepdims=True)
                xc = acc - mu
                var = jnp.mean(xc * xc, axis=-1, keepdims=True)
                y = xc * lax.rsqrt(var + EPS) * lng_ref[...] + lnb_ref[...]
                ys.append(_silu(y) * _silu(z[g * SUBLANES:(g + 1) * SUBLANES]))
            o_ref[pl.ds(rs, pack), :] = jnp.concatenate(ys, axis=0).astype(o_ref.dtype)
        return carry

    lax.fori_loop(0, TS // CONV_LN_ROWS, norm_rows, 0)


def _conv_module(proj, conv_w, conv_b, ln_g, ln_b, B, S, u_off, z_off):
    C = conv_w.shape[-1]
    TS = CONV_TS
    ns = S // TS
    hb = TS // CONV_HALO
    return pl.pallas_call(
        _conv_kernel,
        grid=(B, ns),
        in_specs=[
            pl.BlockSpec((TS, 2 * C), lambda b, s: (b * ns + s, u_off // (2 * C))),
            pl.BlockSpec((CONV_HALO, 2 * C),
                         lambda b, s: (jnp.maximum((b * ns + s) * hb - 1, 0), u_off // (2 * C))),
            pl.BlockSpec((TS, C), lambda b, s: (b * ns + s, z_off // C)),
            pl.BlockSpec((CONV_K, SUBLANES, C), lambda b, s: (0, 0, 0)),
            pl.BlockSpec((SUBLANES, C), lambda b, s: (0, 0)),
            pl.BlockSpec((SUBLANES, C), lambda b, s: (0, 0)),
            pl.BlockSpec((SUBLANES, C), lambda b, s: (0, 0)),
        ],
        out_specs=pl.BlockSpec((TS, C), lambda b, s: (b * ns + s, 0)),
        out_shape=jax.ShapeDtypeStruct((B * S, C), BF16),
        scratch_shapes=[pltpu.VMEM((SUBLANES, CONV_HALO + TS + SUBLANES, C), F32),
                        pltpu.VMEM((TS, C), F32)],
        compiler_params=_cparams(("parallel", "parallel")),
        name="conv_module",
    )(proj, proj, proj, conv_w, conv_b, ln_g, ln_b)


def _memkv_kernel(mem_ref, g_ref, w_ref, o_ref):
    xf = mem_ref[...]
    ms = jnp.mean(xf * xf, axis=-1, keepdims=True)
    mn = (xf * lax.rsqrt(ms + EPS) * g_ref[...]).astype(BF16)
    o_ref[...] = jnp.dot(mn, w_ref[...], preferred_element_type=F32).astype(o_ref.dtype)


def _mem_kv(mem2, g_mem, w_mem_kv, n_mem):
    TM, D = mem2.shape
    N = w_mem_kv.shape[1]
    return pl.pallas_call(
        _memkv_kernel,
        grid=(TM // n_mem,),
        in_specs=[
            pl.BlockSpec((n_mem, D), lambda b: (b, 0)),
            pl.BlockSpec((1, D), lambda b: (0, 0)),
            pl.BlockSpec((D, N), lambda b: (0, 0)),
        ],
        out_specs=pl.BlockSpec((n_mem, N), lambda b: (b, 0)),
        out_shape=jax.ShapeDtypeStruct((TM, N), BF16),
        compiler_params=_cparams(("parallel",)),
        name="mem_kv",
    )(mem2, g_mem, w_mem_kv)


def _xattn_kernel(q_ref, z_ref, kv_ref, o_ref):
    dx = HEAD_DIM_X
    d_x = N_HEADS_X * dx
    for h in range(N_HEADS_X):
        qh = q_ref[:, h * dx:(h + 1) * dx]
        kh = kv_ref[:, h * dx:(h + 1) * dx]
        vh = kv_ref[:, d_x + h * dx:d_x + (h + 1) * dx]
        s = lax.dot_general(qh, kh, (((1,), (1,)), ((), ())), preferred_element_type=F32)
        m = jnp.max(s, axis=-1, keepdims=True)
        p = jnp.exp2(s - m)
        l = jnp.sum(p, axis=-1, keepdims=True)
        o = jnp.dot(p.astype(BF16), vh, preferred_element_type=F32) / l
        z = z_ref[:, h * dx:(h + 1) * dx].astype(F32)
        o_ref[:, h * dx:(h + 1) * dx] = (o * _silu(z)).astype(o_ref.dtype)


def _cross_attention(proj, kv, B, S, n_mem, q_off, z_off):
    d_x = N_HEADS_X * HEAD_DIM_X
    TQ = XATT_TQ
    nq = S // TQ
    return pl.pallas_call(
        _xattn_kernel,
        grid=(B, nq),
        in_specs=[
            pl.BlockSpec((TQ, d_x), lambda b, i: (b * nq + i, q_off // d_x)),
            pl.BlockSpec((TQ, d_x), lambda b, i: (b * nq + i, z_off // d_x)),
            pl.BlockSpec((n_mem, 2 * d_x), lambda b, i: (b, 0)),
        ],
        out_specs=pl.BlockSpec((TQ, d_x), lambda b, i: (b * nq + i, 0)),
        out_shape=jax.ShapeDtypeStruct((B * S, d_x), BF16),
        compiler_params=_cparams(("parallel", "parallel")),
        name="mem_xattn",
    )(proj, proj, kv)


def _merge_kernel(a_ref, b_ref, c_ref, gt_ref, x_ref, woa_ref, wob_ref, woc_ref, wout_ref,
                  bm_ref, gp_ref, o_ref):
    D = x_ref.shape[1]
    merged = jnp.zeros(x_ref.shape, F32)
    for n, (br_ref, w_ref) in enumerate(((a_ref, woa_ref), (b_ref, wob_ref), (c_ref, woc_ref))):
        yb = jnp.dot(br_ref[...], w_ref[...], preferred_element_type=F32)
        gate = _sigmoid(gt_ref[:, n * D:(n + 1) * D].astype(F32) + bm_ref[:, n * D:(n + 1) * D])
        merged = merged + gate * yb
    y = jnp.dot(merged.astype(BF16), wout_ref[...], preferred_element_type=F32)
    ms = jnp.mean(y * y, axis=-1, keepdims=True)
    o_ref[...] = x_ref[...] + y * lax.rsqrt(ms + EPS) * gp_ref[...]


def _merge(ya, yb, yc, proj, x2, w_oa, w_ob, w_oc, w_out, b_merge, g_post, g_off):
    T, D = x2.shape
    TM = MERGE_TM
    row = lambda i: (i, 0)
    fixed = lambda i: (0, 0)
    return pl.pallas_call(
        _merge_kernel,
        grid=(T // TM,),
        in_specs=[
            pl.BlockSpec((TM, D), row),
            pl.BlockSpec((TM, D), row),
            pl.BlockSpec((TM, D), row),
            pl.BlockSpec((TM, N_BRANCH * D), lambda i: (i, g_off // (N_BRANCH * D))),
            pl.BlockSpec((TM, D), row),
            pl.BlockSpec((D, D), fixed),
            pl.BlockSpec((D, D), fixed),
            pl.BlockSpec((D, D), fixed),
            pl.BlockSpec((D, D), fixed),
            pl.BlockSpec((1, N_BRANCH * D), fixed),
            pl.BlockSpec((1, D), fixed),
        ],
        out_specs=pl.BlockSpec((TM, D), row),
        out_shape=jax.ShapeDtypeStruct((T, D), F32),
        compiler_params=_cparams(("parallel",)),
        name="merge_out",
    )(ya, yb, yc, proj, x2, w_oa, w_ob, w_oc, w_out, b_merge, g_post)


def kernel(x, mem, rel_bias, g_pre, g_mem, w_in, b_merge, lam_q1, lam_k1, lam_q2, lam_k2,
           g_subln, w_oa, conv_w, conv_b, ln_g, ln_b, w_ob, w_mem_kv, w_oc, w_out, g_post):
    B, S, D = x.shape
    n_mem = mem.shape[1]
    depth = w_in.shape[0]
    d_a = N_HEADS_A * 2 * HEAD_DIM_A
    d_conv = conv_w.shape[2]
    d_x = N_HEADS_X * HEAD_DIM_X
    sizes = (d_a, d_a, d_a, d_a, 2 * d_conv, d_conv, d_x, d_x, N_BRANCH * D)
    offs = [int(o) for o in np.cumsum((0,) + sizes)]
    assert offs[-1] == w_in.shape[2] and d_a == d_conv == d_x == D
    assert MAX_DIST <= ATT_T and ATT_T % CHUNK == 0 and CONV_HALO >= CONV_K - 1
    assert (B * S) % PROJ_TM == 0 and offs[-1] % PROJ_TN == 0 and (B * S) % MERGE_TM == 0
    assert S % ATT_T == 0 and N_HEADS_A % ATT_NH == 0 and S % CONV_TS == 0 and S % XATT_TQ == 0
    assert CONV_TS % CONV_LN_ROWS == 0 and CONV_TS % CONV_ROWS == 0 and CONV_TS % CONV_HALO == 0
    assert (CONV_HALO + CONV_TS) % CONV_RS == 0

    x2 = x.reshape(B * S, D)
    mem2 = mem.reshape(B * n_mem, D)
    bias_tiles = _bias_tiles(rel_bias)
    col_scale = (jnp.ones((SUBLANES, offs[-1]), F32).at[:, :d_a].set(Q_SCALE_LOG2)
                 .at[:, offs[6]:offs[7]].set(XQ_SCALE_LOG2))
    for l in range(depth):
        lam_init = 0.8 - 0.6 * math.exp(-0.3 * l)
        proj = _proj(x2, g_pre[l][None], w_in[l].astype(BF16), col_scale)

        lamv = jnp.stack([lam_q1[l], lam_k1[l], lam_q2[l], lam_k2[l]])
        ya = _diff_attention(proj, bias_tiles, lamv, g_subln[l][None], B, S, lam_init)

        rep = lambda v: jnp.broadcast_to(v[..., None, :], v.shape[:-1] + (SUBLANES, v.shape[-1]))
        yb = _conv_module(proj, rep(conv_w[l]), rep(conv_b[l]), rep(ln_g[l]), rep(ln_b[l]),
                          B, S, offs[4], offs[5])

        kv = _mem_kv(mem2, g_mem[l][None], w_mem_kv[l].astype(BF16), n_mem)
        yc = _cross_attention(proj, kv, B, S, n_mem, offs[6], offs[7])

        x2 = _merge(ya, yb, yc, proj, x2, w_oa[l].astype(BF16), w_ob[l].astype(BF16),
                    w_oc[l].astype(BF16), w_out[l].astype(BF16), b_merge[l][None], g_post[l][None],
                    offs[8])
    return x2.reshape(B, S, D)
```

```python
import functools
import math

import jax
import jax.numpy as jnp
import numpy as np
from jax import lax
from jax.experimental import pallas as pl
from jax.experimental.pallas import tpu as pltpu

CHUNK = 64
N_HEADS_A = 8
HEAD_DIM_A = 64
N_HEADS_X = 4
HEAD_DIM_X = 256
CONV_K = 31
N_BRANCH = 3
N_BUCKETS = 32
MAX_DIST = 128
EPS = 1e-6
NEG_INF = -1e30
LOG2_E = math.log2(math.e)
Q_SCALE_LOG2 = HEAD_DIM_A ** -0.5 * LOG2_E
XQ_SCALE_LOG2 = HEAD_DIM_X ** -0.5 * LOG2_E

LANES = 128
SUBLANES = 8
VMEM_LIMIT_BYTES = 48 * 1024 * 1024

PROJ_TM = 1024
PROJ_TN = 4096
ATT_T = 256
ATT_NH = 8
ATT_SKEW = 1
CONV_TS = 512
CONV_HALO = 32
CONV_RS = 32
CONV_LN_ROWS = 128
CONV_ROWS = 256
XATT_TQ = 512
MERGE_TM = 512

BF16 = jnp.bfloat16
F32 = jnp.float32


def _cparams(sem):
    return pltpu.CompilerParams(dimension_semantics=sem, vmem_limit_bytes=VMEM_LIMIT_BYTES)


def _sigmoid(z):
    return 0.5 * jnp.tanh(0.5 * z) + 0.5


def _silu(z):
    return z * _sigmoid(z)


def _proj_kernel(x_ref, g_ref, w_ref, cs_ref, o_ref, h_scr):
    @pl.when(pl.program_id(1) == 0)
    def _():
        xf = x_ref[...]
        ms = jnp.mean(xf * xf, axis=-1, keepdims=True)
        h_scr[...] = (xf * lax.rsqrt(ms + EPS) * g_ref[...]).astype(BF16)

    acc = jnp.dot(h_scr[...], w_ref[...], preferred_element_type=F32)
    tm, tn = acc.shape
    acc = acc.reshape(tm // SUBLANES, SUBLANES, tn) * cs_ref[...][None]
    o_ref[...] = acc.reshape(tm, tn).astype(o_ref.dtype)


def _proj(x2, g_pre, w_in, col_scale):
    T, D = x2.shape
    N = w_in.shape[1]
    return pl.pallas_call(
        _proj_kernel,
        grid=(T // PROJ_TM, N // PROJ_TN),
        in_specs=[
            pl.BlockSpec((PROJ_TM, D), lambda i, j: (i, 0)),
            pl.BlockSpec((1, D), lambda i, j: (0, 0)),
            pl.BlockSpec((D, PROJ_TN), lambda i, j: (0, j)),
            pl.BlockSpec((SUBLANES, PROJ_TN), lambda i, j: (0, j)),
        ],
        out_specs=pl.BlockSpec((PROJ_TM, PROJ_TN), lambda i, j: (i, j)),
        out_shape=jax.ShapeDtypeStruct((T, N), BF16),
        scratch_shapes=[pltpu.VMEM((PROJ_TM, D), BF16)],
        compiler_params=_cparams(("parallel", "arbitrary")),
        name="proj",
    )(x2, g_pre, w_in, col_scale)


def _t5_bucket(rel):
    nb = N_BUCKETS // 2
    max_exact = nb // 2
    ret = (rel > 0).astype(jnp.int32) * nb
    n = jnp.abs(rel)
    nf = jnp.maximum(n, 1).astype(jnp.float32)
    large = max_exact + (jnp.log(nf / max_exact) / math.log(MAX_DIST / max_exact)
                         * (nb - max_exact)).astype(jnp.int32)
    large = jnp.minimum(large, nb - 1)
    return ret + jnp.where(n < max_exact, n, large)


def _bucket_tiles():
    kpos = jnp.arange(ATT_T, dtype=jnp.int32)[:, None]
    qpos = jnp.arange(ATT_T, dtype=jnp.int32)[None, :]
    tiles = []
    for d in (0, -1, -2):
        rel = kpos + d * ATT_T - qpos
        tiles.append(_t5_bucket(rel))
    allowed = (kpos // CHUNK) <= (qpos // CHUNK)
    tiles[0] = jnp.where(allowed, tiles[0], -1)
    return jnp.stack(tiles)


def _bias_kernel(relb_ref, bkt_ref, o_ref):
    h = pl.program_id(0)
    bkt = bkt_ref[...]
    res = jnp.zeros(bkt.shape, F32)
    for b in range(N_BUCKETS):
        res = jnp.where(bkt == b, relb_ref[b, h], res)
    near = res[0:2] - res[2, 0:1, 0:1]
    o_ref[0] = jnp.where(bkt[0:2] < 0, NEG_INF, near * LOG2_E)


def _bias_tiles(rel_bias):
    bkt = _bucket_tiles()
    return pl.pallas_call(
        _bias_kernel,
        grid=(N_HEADS_A,),
        in_specs=[
            pl.BlockSpec(memory_space=pltpu.SMEM),
            pl.BlockSpec((3, ATT_T, ATT_T), lambda h: (0, 0, 0)),
        ],
        out_specs=pl.BlockSpec((1, 2, ATT_T, ATT_T), lambda h: (h, 0, 0, 0)),
        out_shape=jax.ShapeDtypeStruct((N_HEADS_A, 2, ATT_T, ATT_T), F32),
        compiler_params=_cparams(("arbitrary",)),
        name="bias_tiles",
    )(rel_bias, bkt)


def _attn_kernel(lamv_ref, gs_ref, q_ref, qn_ref, k_ref, v_ref, z_ref, bias_ref, o_ref,
                 qq_scr, sta_scr, stb_scr, acc_scr, m_scr, l_scr, *, lam_init):
    T = ATT_T
    qi = pl.program_id(2)
    ones = jnp.ones((2 * SUBLANES, T), BF16)

    all_heads = tuple(range(ATT_NH))

    def scores(t, st_ref, heads=all_heads):
        k0 = pl.multiple_of(t * T, T)
        for hh in heads:
            kb = k_ref[pl.ds(k0, T), hh * LANES:(hh + 1) * LANES]
            st_ref[hh] = jnp.dot(kb, qq_scr[hh], preferred_element_type=F32)

    def start_block(src_ref):
        lane = lax.broadcasted_iota(jnp.int32, (T, LANES), 1)
        for hh in range(ATT_NH):
            q = src_ref[:, hh * LANES:(hh + 1) * LANES]
            zero = jnp.zeros_like(q)
            qq_scr[hh, :, 0:T] = jnp.where(lane < HEAD_DIM_A, q, zero).T
            qq_scr[hh, :, T:2 * T] = jnp.where(lane >= HEAD_DIM_A, q, zero).T
        m_scr[...] = jnp.full(m_scr.shape, -jnp.inf, F32)
        l_scr[...] = jnp.zeros(l_scr.shape, F32)
        acc_scr[...] = jnp.zeros(acc_scr.shape, F32)
        scores(0, sta_scr)

    @pl.when(qi == 0)
    def _():
        start_block(q_ref)

    def softmax_pv(t, st_ref, tile, heads=all_heads):
        k0 = pl.multiple_of(t * T, T)
        for hh in heads:
            st = st_ref[hh]
            if tile is not None:
                bias = bias_ref[hh, tile]
                st = st + jnp.concatenate([bias, bias], axis=1)
            m_prev = m_scr[hh]
            m_new = jnp.maximum(m_prev, jnp.max(st, axis=0, keepdims=True))
            alpha = jnp.exp2(m_prev - m_new)
            p = jnp.exp2(st - m_new).astype(BF16)
            vb = v_ref[pl.ds(k0, T), hh * LANES:(hh + 1) * LANES]
            pv = jnp.dot(jnp.concatenate([vb.T, ones], axis=0), p, preferred_element_type=F32)
            l_scr[hh] = alpha * l_scr[hh] + pv[2 * HEAD_DIM_A:2 * HEAD_DIM_A + 1]
            acc_scr[hh] = alpha * acc_scr[hh] + pv[:2 * HEAD_DIM_A]
            m_scr[hh] = m_new

    def tick(t, cur, nxt, tile):
        scores(t + 1, nxt, all_heads[:ATT_SKEW])
        for hh in all_heads:
            scores(t + 1, nxt, all_heads[hh + ATT_SKEW:hh + ATT_SKEW + 1])
            softmax_pv(t, cur, tile, (hh,))

    n_far = jnp.maximum(qi - 1, 0)

    def far_pair(i, carry):
        tick(2 * i, sta_scr, stb_scr, None)
        tick(2 * i + 1, stb_scr, sta_scr, None)
        return carry

    lax.fori_loop(0, n_far // 2, far_pair, 0)
    odd = lax.rem(n_far, 2) == 1

    @pl.when(odd)
    def _():
        tick(n_far - 1, sta_scr, stb_scr, None)

    @pl.when(jnp.logical_and(qi >= 1, jnp.logical_not(odd)))
    def _():
        tick(qi - 1, sta_scr, stb_scr, 1)
        softmax_pv(qi, stb_scr, 0)

    @pl.when(jnp.logical_and(qi >= 1, odd))
    def _():
        tick(qi - 1, stb_scr, sta_scr, 1)
        softmax_pv(qi, sta_scr, 0)

    @pl.when(qi == 0)
    def _():
        softmax_pv(0, sta_scr, 0)

    lv = lamv_ref[...]
    lam = (jnp.exp(jnp.sum(lv[0:1] * lv[1:2], axis=-1, keepdims=True))
           - jnp.exp(jnp.sum(lv[2:3] * lv[3:4], axis=-1, keepdims=True)) + lam_init)
    for hh in range(ATT_NH):
        ot = acc_scr[hh] / l_scr[hh]
        ot = ot[:, :T] - lam * ot[:, T:]
        ms = jnp.mean(ot * ot, axis=0, keepdims=True)
        on = (ot * lax.rsqrt(ms + EPS)).T * gs_ref[...] * (1.0 - lam_init)
        z = z_ref[:, hh * LANES:(hh + 1) * LANES].astype(F32)
        o_ref[:, hh * LANES:(hh + 1) * LANES] = (on * _silu(z)).astype(o_ref.dtype)

    start_block(qn_ref)


def _diff_attention(proj, bias_tiles, lamv, g_subln, B, S, lam_init):
    T = ATT_T
    nq = S // T
    H = N_HEADS_A
    d_a = H * 2 * HEAD_DIM_A
    W = ATT_NH * LANES
    G = H // ATT_NH
    return pl.pallas_call(
        functools.partial(_attn_kernel, lam_init=lam_init),
        grid=(B, G, nq),
        in_specs=[
            pl.BlockSpec((4, HEAD_DIM_A), lambda b, g, i: (0, 0)),
            pl.BlockSpec((1, LANES), lambda b, g, i: (0, 0)),
            pl.BlockSpec((T, W), lambda b, g, i: (b * nq + i, g)),
            pl.BlockSpec((T, W), lambda b, g, i: (b * nq + jnp.minimum(i + 1, nq - 1), g)),
            pl.BlockSpec((S, W), lambda b, g, i: (b, G + g)),
            pl.BlockSpec((S, W), lambda b, g, i: (b, 2 * G + g)),
            pl.BlockSpec((T, W), lambda b, g, i: (b * nq + i, 3 * G + g)),
            pl.BlockSpec((ATT_NH, 2, T, T), lambda b, g, i: (g, 0, 0, 0)),
        ],
        out_specs=pl.BlockSpec((T, W), lambda b, g, i: (b * nq + i, g)),
        out_shape=jax.ShapeDtypeStruct((B * S, d_a), BF16),
        scratch_shapes=[pltpu.VMEM((ATT_NH, LANES, 2 * T), BF16),
                        pltpu.VMEM((ATT_NH, T, 2 * T), F32),
                        pltpu.VMEM((ATT_NH, T, 2 * T), F32),
                        pltpu.VMEM((ATT_NH, 2 * HEAD_DIM_A, 2 * T), F32),
                        pltpu.VMEM((ATT_NH, 1, 2 * T), F32),
                        pltpu.VMEM((ATT_NH, 1, 2 * T), F32)],
        compiler_params=_cparams(("parallel", "parallel", "arbitrary")),
        name="diff_attn",
    )(lamv, g_subln, proj, proj, proj, proj, proj, bias_tiles)


def _conv_kernel(u_ref, halo_ref, z_ref, cw_ref, cb_ref, lng_ref, lnb_ref, o_ref, c_scr, y_scr):
    C = o_ref.shape[1]
    TS = o_ref.shape[0]

    def glu(u):
        return u[:, :C].astype(F32) * _sigmoid(u[:, C:].astype(F32))

    halo = glu(halo_ref[...])
    c_scr[0, 0:CONV_HALO] = jnp.where(pl.program_id(1) == 0, jnp.zeros_like(halo), halo)
    c_scr[0, CONV_HALO:CONV_HALO + TS] = glu(u_ref[...])
    c_scr[0, CONV_HALO + TS:] = jnp.zeros((SUBLANES, C), F32)

    def shift(r, carry):
        r0 = pl.multiple_of(r * CONV_RS, CONV_RS)
        win = c_scr[0, pl.ds(r0, CONV_RS + SUBLANES), :]
        for s in range(1, SUBLANES):
            c_scr[s, pl.ds(r0, CONV_RS), :] = win[s:s + CONV_RS]
        return carry

    lax.fori_loop(0, (CONV_HALO + TS) // CONV_RS, shift, 0)

    base = CONV_HALO - (CONV_K - 1)

    for lb in range(C // LANES):
        lanes = slice(lb * LANES, (lb + 1) * LANES)
        taps = [cw_ref[j, :, lanes] for j in range(CONV_K)]
        bias = cb_ref[:, lanes]

        def conv_rows(r, carry, lanes=lanes, taps=taps, bias=bias):
            r0 = pl.multiple_of(r * CONV_ROWS, CONV_ROWS)
            for g in range(CONV_ROWS // SUBLANES):
                acc = bias
                for j in range(CONV_K):
                    off = base + j
                    row = r0 + g * SUBLANES + off - off % SUBLANES
                    acc = acc + taps[j] * c_scr[off % SUBLANES, pl.ds(row, SUBLANES), lanes]
                y_scr[pl.ds(r0 + g * SUBLANES, SUBLANES), lanes] = acc
            return carry

        lax.fori_loop(0, TS // CONV_ROWS, conv_rows, 0)

    pack = 2 * SUBLANES

    def norm_rows(r, carry):
        r0 = pl.multiple_of(r * CONV_LN_ROWS, CONV_LN_ROWS)
        for s in range(CONV_LN_ROWS // pack):
            rs = r0 + s * pack
            z = z_ref[pl.ds(rs, pack), :].astype(F32)
            ys = []
            for g in range(pack // SUBLANES):
                acc = y_scr[pl.ds(rs + g * SUBLANES, SUBLANES), :]
                mu = jnp.mean(acc, axis=-1, keepdims=True)
                xc = acc - mu
                var = jnp.mean(xc * xc, axis=-1, keepdims=True)
                y = xc * lax.rsqrt(var + EPS) * lng_ref[...] + lnb_ref[...]
                ys.append(_silu(y) * _silu(z[g * SUBLANES:(g + 1) * SUBLANES]))
            o_ref[pl.ds(rs, pack), :] = jnp.concatenate(ys, axis=0).astype(o_ref.dtype)
        return carry

    lax.fori_loop(0, TS // CONV_LN_ROWS, norm_rows, 0)


def _conv_module(proj, conv_w, conv_b, ln_g, ln_b, B, S, u_off, z_off):
    C = conv_w.shape[-1]
    TS = CONV_TS
    ns = S // TS
    hb = TS // CONV_HALO
    return pl.pallas_call(
        _conv_kernel,
        grid=(B, ns),
        in_specs=[
            pl.BlockSpec((TS, 2 * C), lambda b, s: (b * ns + s, u_off // (2 * C))),
            pl.BlockSpec((CONV_HALO, 2 * C),
                         lambda b, s: (jnp.maximum((b * ns + s) * hb - 1, 0), u_off // (2 * C))),
            pl.BlockSpec((TS, C), lambda b, s: (b * ns + s, z_off // C)),
            pl.BlockSpec((CONV_K, SUBLANES, C), lambda b, s: (0, 0, 0)),
            pl.BlockSpec((SUBLANES, C), lambda b, s: (0, 0)),
            pl.BlockSpec((SUBLANES, C), lambda b, s: (0, 0)),
            pl.BlockSpec((SUBLANES, C), lambda b, s: (0, 0)),
        ],
        out_specs=pl.BlockSpec((TS, C), lambda b, s: (b * ns + s, 0)),
        out_shape=jax.ShapeDtypeStruct((B * S, C), BF16),
        scratch_shapes=[pltpu.VMEM((SUBLANES, CONV_HALO + TS + SUBLANES, C), F32),
                        pltpu.VMEM((TS, C), F32)],
        compiler_params=_cparams(("parallel", "parallel")),
        name="conv_module",
    )(proj, proj, proj, conv_w, conv_b, ln_g, ln_b)


def _memkv_kernel(mem_ref, g_ref, w_ref, o_ref, kt_ref):
    xf = mem_ref[...]
    ms = jnp.mean(xf * xf, axis=-1, keepdims=True)
    mn = (xf * lax.rsqrt(ms + EPS) * g_ref[...]).astype(BF16)
    kv = jnp.dot(mn, w_ref[...], preferred_element_type=F32)
    o_ref[...] = kv.astype(o_ref.dtype)
    kt_ref[...] = kv[:, :kt_ref.shape[0]].T.astype(kt_ref.dtype)


def _mem_kv(mem2, g_mem, w_mem_kv, n_mem):
    TM, D = mem2.shape
    N = w_mem_kv.shape[1]
    return pl.pallas_call(
        _memkv_kernel,
        grid=(TM // n_mem,),
        in_specs=[
            pl.BlockSpec((n_mem, D), lambda b: (b, 0)),
            pl.BlockSpec((1, D), lambda b: (0, 0)),
            pl.BlockSpec((D, N), lambda b: (0, 0)),
        ],
        out_specs=[pl.BlockSpec((n_mem, N), lambda b: (b, 0)),
                   pl.BlockSpec((N // 2, n_mem), lambda b: (b, 0))],
        out_shape=[jax.ShapeDtypeStruct((TM, N), BF16),
                   jax.ShapeDtypeStruct((TM // n_mem * (N // 2), n_mem), BF16)],
        compiler_params=_cparams(("parallel",)),
        name="mem_kv",
    )(mem2, g_mem, w_mem_kv)


def _xattn_kernel(q_ref, z_ref, kv_ref, kt_ref, o_ref):
    dx = HEAD_DIM_X
    d_x = N_HEADS_X * dx
    for h in range(N_HEADS_X):
        qh = q_ref[:, h * dx:(h + 1) * dx]
        kht = kt_ref[h * dx:(h + 1) * dx, :]
        vh = kv_ref[:, d_x + h * dx:d_x + (h + 1) * dx]
        s = jnp.dot(qh, kht, preferred_element_type=F32)
        m = jnp.max(s, axis=-1, keepdims=True)
        p = jnp.exp2(s - m)
        l = jnp.sum(p, axis=-1, keepdims=True)
        o = jnp.dot(p.astype(BF16), vh, preferred_element_type=F32) / l
        z = z_ref[:, h * dx:(h + 1) * dx].astype(F32)
        o_ref[:, h * dx:(h + 1) * dx] = (o * _silu(z)).astype(o_ref.dtype)


def _cross_attention(proj, kv, kt, B, S, n_mem, q_off, z_off):
    d_x = N_HEADS_X * HEAD_DIM_X
    TQ = XATT_TQ
    nq = S // TQ
    return pl.pallas_call(
        _xattn_kernel,
        grid=(B, nq),
        in_specs=[
            pl.BlockSpec((TQ, d_x), lambda b, i: (b * nq + i, q_off // d_x)),
            pl.BlockSpec((TQ, d_x), lambda b, i: (b * nq + i, z_off // d_x)),
            pl.BlockSpec((n_mem, 2 * d_x), lambda b, i: (b, 0)),
            pl.BlockSpec((d_x, n_mem), lambda b, i: (b, 0)),
        ],
        out_specs=pl.BlockSpec((TQ, d_x), lambda b, i: (b * nq + i, 0)),
        out_shape=jax.ShapeDtypeStruct((B * S, d_x), BF16),
        compiler_params=_cparams(("parallel", "parallel")),
        name="mem_xattn",
    )(proj, proj, kv, kt)


def _merge_kernel(a_ref, b_ref, c_ref, gt_ref, x_ref, woa_ref, wob_ref, woc_ref, wout_ref,
                  bm_ref, gp_ref, o_ref):
    D = x_ref.shape[1]
    merged = jnp.zeros(x_ref.shape, F32)
    for n, (br_ref, w_ref) in enumerate(((a_ref, woa_ref), (b_ref, wob_ref), (c_ref, woc_ref))):
        yb = jnp.dot(br_ref[...], w_ref[...], preferred_element_type=F32)
        gate = _sigmoid(gt_ref[:, n * D:(n + 1) * D].astype(F32) + bm_ref[:, n * D:(n + 1) * D])
        merged = merged + gate * yb
    y = jnp.dot(merged.astype(BF16), wout_ref[...], preferred_element_type=F32)
    ms = jnp.mean(y * y, axis=-1, keepdims=True)
    o_ref[...] = x_ref[...] + y * lax.rsqrt(ms + EPS) * gp_ref[...]


def _merge(ya, yb, yc, proj, x2, w_oa, w_ob, w_oc, w_out, b_merge, g_post, g_off):
    T, D = x2.shape
    TM = MERGE_TM
    row = lambda i: (i, 0)
    fixed = lambda i: (0, 0)
    return pl.pallas_call(
        _merge_kernel,
        grid=(T // TM,),
        in_specs=[
            pl.BlockSpec((TM, D), row),
            pl.BlockSpec((TM, D), row),
            pl.BlockSpec((TM, D), row),
            pl.BlockSpec((TM, N_BRANCH * D), lambda i: (i, g_off // (N_BRANCH * D))),
            pl.BlockSpec((TM, D), row),
            pl.BlockSpec((D, D), fixed),
            pl.BlockSpec((D, D), fixed),
            pl.BlockSpec((D, D), fixed),
            pl.BlockSpec((D, D), fixed),
            pl.BlockSpec((1, N_BRANCH * D), fixed),
            pl.BlockSpec((1, D), fixed),
        ],
        out_specs=pl.BlockSpec((TM, D), row),
        out_shape=jax.ShapeDtypeStruct((T, D), F32),
        compiler_params=_cparams(("parallel",)),
        name="merge_out",
    )(ya, yb, yc, proj, x2, w_oa, w_ob, w_oc, w_out, b_merge, g_post)


def kernel(x, mem, rel_bias, g_pre, g_mem, w_in, b_merge, lam_q1, lam_k1, lam_q2, lam_k2,
           g_subln, w_oa, conv_w, conv_b, ln_g, ln_b, w_ob, w_mem_kv, w_oc, w_out, g_post):
    B, S, D = x.shape
    n_mem = mem.shape[1]
    depth = w_in.shape[0]
    d_a = N_HEADS_A * 2 * HEAD_DIM_A
    d_conv = conv_w.shape[2]
    d_x = N_HEADS_X * HEAD_DIM_X
    sizes = (d_a, d_a, d_a, d_a, 2 * d_conv, d_conv, d_x, d_x, N_BRANCH * D)
    offs = [int(o) for o in np.cumsum((0,) + sizes)]
    assert offs[-1] == w_in.shape[2] and d_a == d_conv == d_x == D
    assert MAX_DIST <= ATT_T and ATT_T % CHUNK == 0 and CONV_HALO >= CONV_K - 1
    assert (B * S) % PROJ_TM == 0 and offs[-1] % PROJ_TN == 0 and (B * S) % MERGE_TM == 0
    assert S % ATT_T == 0 and N_HEADS_A % ATT_NH == 0 and S % CONV_TS == 0 and S % XATT_TQ == 0
    assert CONV_TS % CONV_LN_ROWS == 0 and CONV_TS % CONV_ROWS == 0 and CONV_TS % CONV_HALO == 0
    assert (CONV_HALO + CONV_TS) % CONV_RS == 0

    x2 = x.reshape(B * S, D)
    mem2 = mem.reshape(B * n_mem, D)
    bias_tiles = _bias_tiles(rel_bias)
    col_scale = (jnp.ones((SUBLANES, offs[-1]), F32).at[:, :d_a].set(Q_SCALE_LOG2)
                 .at[:, offs[6]:offs[7]].set(XQ_SCALE_LOG2))
    for l in range(depth):
        lam_init = 0.8 - 0.6 * math.exp(-0.3 * l)
        proj = _proj(x2, g_pre[l][None], w_in[l].astype(BF16), col_scale)

        lamv = jnp.stack([lam_q1[l], lam_k1[l], lam_q2[l], lam_k2[l]])
        ya = _diff_attention(proj, bias_tiles, lamv, g_subln[l][None], B, S, lam_init)

        rep = lambda v: jnp.broadcast_to(v[..., None, :], v.shape[:-1] + (SUBLANES, v.shape[-1]))
        yb = _conv_module(proj, rep(conv_w[l]), rep(conv_b[l]), rep(ln_g[l]), rep(ln_b[l]),
                          B, S, offs[4], offs[5])

        kv, kt = _mem_kv(mem2, g_mem[l][None], w_mem_kv[l].astype(BF16), n_mem)
        yc = _cross_attention(proj, kv, kt, B, S, n_mem, offs[6], offs[7])

        x2 = _merge(ya, yb, yc, proj, x2, w_oa[l].astype(BF16), w_ob[l].astype(BF16),
                    w_oc[l].astype(BF16), w_out[l].astype(BF16), b_merge[l][None], g_post[l][None],
                    offs[8])
    return x2.reshape(B, S, D)
```
